```python
import jax, jax.numpy as jnp
from jax import lax
import numpy as np

D_MODEL = 1024
BATCH = 2
SEQ = 8192
DEPTH = 4

N_META = 16
HEAD_DIM = 64
N_HEADS = D_MODEL // HEAD_DIM
N_A_LAYERS = DEPTH // 2
N_B_LAYERS = DEPTH - N_A_LAYERS
DECAY_LORA = 64
ICLR_LORA = 64
VRES_LORA = 32
Q_BLOCK = 128
NORM_EPS = 1e-6
GN_EPS = 64e-5
A_IN_WIDTH = 4 * D_MODEL + DECAY_LORA + ICLR_LORA

kernel_name = "yoco_rwkv7_fox_hybrid"


def rmsnorm(x, g):
    xf = x.astype(jnp.float32)
    xf = xf * lax.rsqrt(jnp.mean(xf * xf, axis=-1, keepdims=True) + NORM_EPS)
    return xf.astype(x.dtype) * g


def token_shift(u, mu):
    prev = jnp.pad(u, ((0, 0), (1, 0), (0, 0)))[:, :-1]
    return u + mu * (prev - u)


def wkv7_scan(r, decay, k, v, kk, a):
    B, T, H, N = r.shape
    def step(S, inp):
        r_t, w_t, k_t, v_t, kk_t, a_t = inp
        sa = jnp.einsum('bhij,bhj->bhi', S, -kk_t)
        S = (S * w_t[:, :, None, :]
             + sa[..., None] * (kk_t * a_t)[:, :, None, :]
             + v_t[..., None] * k_t[:, :, None, :])
        y = jnp.einsum('bhij,bhj->bhi', S, r_t)
        return S, y
    xs = tuple(jnp.moveaxis(t.astype(jnp.float32), 1, 0) for t in (r, decay, k, v, kk, a))
    S0 = jnp.zeros((B, H, N, N), jnp.float32)
    _, ys = lax.scan(step, S0, xs)
    return jnp.moveaxis(ys, 0, 1)


def rwkv7_layer(h, v_first, norm, w_in, mu, decay_up, decay_bias, iclr_up, iclr_bias,
                vres_up, vres_bias, k_k, k_a, r_k, gn_w, gn_b, w_out):
    B, T, D = h.shape
    H, N = N_HEADS, HEAD_DIM
    xn = rmsnorm(h, norm)
    proj = token_shift(xn @ w_in, mu)
    r = proj[..., :D]
    k = proj[..., D:2 * D]
    v = proj[..., 2 * D:3 * D]
    z = proj[..., 3 * D:4 * D]
    w_lo = proj[..., 4 * D:4 * D + DECAY_LORA]
    a_lo = proj[..., 4 * D + DECAY_LORA:4 * D + DECAY_LORA + ICLR_LORA]
    w_log = -jax.nn.softplus(-(decay_bias + jnp.tanh(w_lo) @ decay_up).astype(jnp.float32)) - 0.5
    decay = jnp.exp(-jnp.exp(w_log))
    a = jax.nn.sigmoid(iclr_bias + a_lo @ iclr_up)
    if v_first is None:
        v_first = v
    else:
        vres_lo = proj[..., 4 * D + DECAY_LORA + ICLR_LORA:]
        v = v + (v_first - v) * jax.nn.sigmoid(vres_bias + vres_lo @ vres_up)
    kk = (k * k_k).reshape(B, T, H, N)
    kkf = kk.astype(jnp.float32)
    kk = (kkf / jnp.maximum(jnp.sqrt(jnp.sum(kkf * kkf, axis=-1, keepdims=True)), 1e-12)).astype(k.dtype)
    k = k * (1 + (a - 1) * k_a)
    rh = r.reshape(B, T, H, N)
    kh = k.reshape(B, T, H, N)
    vh = v.reshape(B, T, H, N)
    y = wkv7_scan(rh, decay.reshape(B, T, H, N), kh, vh, kk, a.reshape(B, T, H, N))
    mean = jnp.mean(y, axis=-1, keepdims=True)
    var = jnp.mean(jnp.square(y - mean), axis=-1, keepdims=True)
    y = ((y - mean) * lax.rsqrt(var + GN_EPS)).astype(h.dtype).reshape(B, T, D) * gn_w + gn_b
    bonus = jnp.sum(rh * kh * r_k.reshape(H, N), axis=-1, keepdims=True) * vh
    out = (y + bonus.reshape(B, T, D)) * jax.nn.silu(z)
    return h + out @ w_out, v_first


def head_rmsnorm(x, g):
    xf = x.astype(jnp.float32)
    xf = xf * lax.rsqrt(jnp.mean(xf * xf, axis=-1, keepdims=True) + NORM_EPS)
    return xf.astype(x.dtype) * g


def shared_kv(h, kv_norm, kv_w, kv_f_bias, k_norm):
    B, T, D = h.shape
    kvf = rmsnorm(h, kv_norm) @ kv_w
    k_sh = head_rmsnorm(kvf[..., :D].reshape(B, T, N_HEADS, HEAD_DIM), k_norm)
    v_sh = kvf[..., D:2 * D].reshape(B, T, N_HEADS, HEAD_DIM)
    log_f = jax.nn.log_sigmoid(kvf[..., 2 * D:].astype(jnp.float32) + kv_f_bias)
    c_all = jnp.cumsum(log_f, axis=1)
    return k_sh, v_sh, c_all


def fox_attention(q, k, v, c_all):
    B, S, H, N = q.shape
    T = k.shape[1]
    n_blk = S // Q_BLOCK
    scale = HEAD_DIM ** -0.5
    q_blk = q.reshape(B, n_blk, Q_BLOCK, H, N).transpose(1, 0, 3, 2, 4)
    cq_blk = c_all[:, N_META:].reshape(B, n_blk, Q_BLOCK, H).transpose(1, 0, 3, 2)
    kh = k.transpose(0, 2, 1, 3)
    vh = v.transpose(0, 2, 1, 3)
    ck = c_all.transpose(0, 2, 1)
    key_pos = jnp.arange(T)

    def one_block(args):
        qb, cqb, blk = args
        q_pos = N_META + blk * Q_BLOCK + jnp.arange(Q_BLOCK)
        logits = (jnp.einsum('bhqn,bhkn->bhqk', qb, kh).astype(jnp.float32) * scale
                  + cqb[..., None] - ck[:, :, None, :])
        mask = key_pos[None, :] <= q_pos[:, None]
        logits = jnp.where(mask, logits, -jnp.inf)
        p = jax.nn.softmax(logits, axis=-1)
        return jnp.einsum('bhqk,bhkn->bhqn', p.astype(vh.dtype), vh)

    out = lax.map(one_block, (q_blk, cq_blk, jnp.arange(n_blk)))
    return out.transpose(1, 0, 3, 2, 4).reshape(B, S, H, N)


def fox_layer(h, k_sh, v_sh, c_all, norm, w_in, q_norm, w_out):
    B, S, D = h.shape
    proj = rmsnorm(h, norm) @ w_in
    q = head_rmsnorm(proj[..., :D].reshape(B, S, N_HEADS, HEAD_DIM), q_norm)
    z = proj[..., D:]
    attn = fox_attention(q, k_sh, v_sh, c_all).reshape(B, S, D)
    return h + (attn * jax.nn.silu(z)) @ w_out


def setup_inputs(seed: int = 0) -> dict:
    key = jax.random.key(seed)
    ks = iter(jax.random.split(key, 40))
    D, H, NA, NB = D_MODEL, N_HEADS, N_A_LAYERS, N_B_LAYERS

    def nrm(shape, scale):
        return scale * jax.random.normal(next(ks), shape, jnp.float32)

    def unif(shape):
        return jax.random.uniform(next(ks), shape, jnp.float32, 0.05, 0.95)

    return {
        "x": nrm((BATCH, SEQ, D), 1.0),
        "meta_tokens": nrm((N_META, D), 1.0),
        "a_norm": 1.0 + nrm((NA, D), 0.1),
        "a_w_in": nrm((NA, D, A_IN_WIDTH), D ** -0.5),
        "a_shift_mu": unif((NA, A_IN_WIDTH)),
        "a_vres_down": nrm((NA - 1, D, VRES_LORA), D ** -0.5),
        "a_vres_mu": unif((NA - 1, VRES_LORA)),
        "a_vres_up": nrm((NA - 1, VRES_LORA, D), VRES_LORA ** -0.5),
        "a_vres_bias": 0.5 + nrm((NA - 1, D), 0.1),
        "a_decay_up": nrm((NA, DECAY_LORA, D), 0.5 * DECAY_LORA ** -0.5),
        "a_decay_bias": -0.5 + nrm((NA, D), 0.5),
        "a_iclr_up": nrm((NA, ICLR_LORA, D), ICLR_LORA ** -0.5),
        "a_iclr_bias": nrm((NA, D), 0.1),
        "a_k_k": 0.85 + nrm((NA, D), 0.05),
        "a_k_a": 1.0 + nrm((NA, D), 0.05),
        "a_r_k": nrm((NA, D), 0.1),
        "a_gn_w": 1.0 + nrm((NA, D), 0.1),
        "a_gn_b": nrm((NA, D), 0.02),
        "a_w_out": nrm((NA, D, D), 0.5 * D ** -0.5),
        "kv_norm": 1.0 + nrm((D,), 0.1),
        "kv_w": nrm((D, 2 * D + H), D ** -0.5),
        "kv_f_bias": 3.0 + nrm((H,), 0.5),
        "k_norm": 1.0 + nrm((HEAD_DIM,), 0.1),
        "b_norm": 1.0 + nrm((NB, D), 0.1),
        "b_w_in": nrm((NB, D, 2 * D), D ** -0.5),
        "b_q_norm": 1.0 + nrm((NB, HEAD_DIM), 0.1),
        "b_w_out": nrm((NB, D, D), 0.5 * D ** -0.5),
    }


def reference(x, meta_tokens, a_norm, a_w_in, a_shift_mu, a_vres_down, a_vres_mu, a_vres_up,
              a_vres_bias, a_decay_up, a_decay_bias, a_iclr_up, a_iclr_bias, a_k_k, a_k_a, a_r_k,
              a_gn_w, a_gn_b, a_w_out, kv_norm, kv_w, kv_f_bias, k_norm, b_norm, b_w_in,
              b_q_norm, b_w_out):
    B = x.shape[0]
    meta = jnp.broadcast_to(meta_tokens.astype(x.dtype)[None], (B, N_META, D_MODEL))
    h = jnp.concatenate([meta, x], axis=1)
    v_first = None
    k_sh = v_sh = c_all = None
    for layer in range(DEPTH):
        if layer < N_A_LAYERS:
            l = layer
            if l == 0:
                w_in, mu, vres_up, vres_bias = a_w_in[0], a_shift_mu[0], None, None
            else:
                w_in = jnp.concatenate([a_w_in[l], a_vres_down[l - 1]], axis=1)
                mu = jnp.concatenate([a_shift_mu[l], a_vres_mu[l - 1]], axis=0)
                vres_up, vres_bias = a_vres_up[l - 1], a_vres_bias[l - 1]
            h, v_first = rwkv7_layer(h, v_first, a_norm[l], w_in, mu, a_decay_up[l], a_decay_bias[l],
                                     a_iclr_up[l], a_iclr_bias[l], vres_up, vres_bias, a_k_k[l],
                                     a_k_a[l], a_r_k[l], a_gn_w[l], a_gn_b[l], a_w_out[l])
            if layer == N_A_LAYERS - 1:
                k_sh, v_sh, c_all = shared_kv(h, kv_norm, kv_w, kv_f_bias, k_norm)
                h = h[:, N_META:]
        else:
            j = layer - N_A_LAYERS
            h = fox_layer(h, k_sh, v_sh, c_all, b_norm[j], b_w_in[j], b_q_norm[j], b_w_out[j])
    return h
```

```python
import functools

import jax
import jax.numpy as jnp
from jax import lax
from jax.experimental import pallas as pl
from jax.experimental.pallas import tpu as pltpu

N_META = 16
HEAD_DIM = 64
NORM_EPS = 1e-6
GN_EPS = 64e-5
LANES = 128
CHUNK = 64
FRONT = 128
VMEM_LIMIT = 56 * 1024 * 1024

F32 = jnp.float32
BF16 = jnp.bfloat16
_NT = (((1,), (1,)), ((), ()))


def _dot(a, b):
    return jnp.dot(a, b, preferred_element_type=F32)


def _dot_nt(a, b):
    return lax.dot_general(a, b, _NT, preferred_element_type=F32)


def _split2(x):
    hi = x.astype(BF16)
    lo = (x - hi.astype(F32)).astype(BF16)
    return hi, lo


def _split3(x):
    hi = x.astype(BF16)
    r1 = x - hi.astype(F32)
    mid = r1.astype(BF16)
    lo = (r1 - mid.astype(F32)).astype(BF16)
    return hi, mid, lo


def _sigmoid(x):
    return 1.0 / (1.0 + jnp.exp(-x))


def _softplus(x):
    return jnp.maximum(x, 0.0) + jnp.log(1.0 + jnp.exp(-jnp.abs(x)))


def _pick_tile(n, cap, mult):
    best = mult
    for t in range(mult, cap + 1, mult):
        if n % t == 0:
            best = t
    return best


def _cparams(n_axes):
    return pltpu.CompilerParams(dimension_semantics=("arbitrary",) * n_axes,
                                vmem_limit_bytes=VMEM_LIMIT)


def _proj_shift_kernel(x_ref, g_ref, w_ref, mu_ref, o_ref, carry_ref):
    i = pl.program_id(2)
    x = x_ref[0]
    ms = jnp.mean(x * x, axis=-1, keepdims=True)
    xn = (x * lax.rsqrt(ms + NORM_EPS) * g_ref[...]).astype(BF16)
    p = _dot(xn, w_ref[...])
    tm = p.shape[0]

    @pl.when(i == 0)
    def _():
        carry_ref[...] = jnp.zeros_like(carry_ref)

    prev = pltpu.roll(p, 1, axis=0)
    row = lax.broadcasted_iota(jnp.int32, p.shape, 0)
    prev = jnp.where(row == 0, carry_ref[0:1, :], prev)
    o_ref[0] = p + mu_ref[...] * (prev - p)
    carry_ref[0:1, :] = p[tm - 1:tm, :]


def _proj_shift(h, g, w, mu, tn):
    B, T, D = h.shape
    NW = w.shape[1]
    tm = _pick_tile(T, 640, 64)
    return pl.pallas_call(
        _proj_shift_kernel,
        grid=(NW // tn, B, T // tm),
        in_specs=[
            pl.BlockSpec((1, tm, D), lambda j, b, i: (b, i, 0)),
            pl.BlockSpec((1, D), lambda j, b, i: (0, 0)),
            pl.BlockSpec((D, tn), lambda j, b, i: (0, j)),
            pl.BlockSpec((1, tn), lambda j, b, i: (0, j)),
        ],
        out_specs=pl.BlockSpec((1, tm, tn), lambda j, b, i: (b, i, j)),
        out_shape=jax.ShapeDtypeStruct((B, T, NW), F32),
        scratch_shapes=[pltpu.VMEM((8, tn), F32)],
        compiler_params=_cparams(3),
        name="proj_shift",
    )(h, g.reshape(1, D), w, mu.reshape(1, NW))


def _wkv_kernel(*refs, has_vres, n_chunks):
    if has_vres:
        (r_ref, k_ref, v_ref, z_ref, wl_ref, al_ref, vl_ref, vf_ref,
         dup_ref, iup_ref, vup_ref, vec_ref, o_ref, h_ref) = refs
    else:
        (r_ref, k_ref, v_ref, z_ref, wl_ref, al_ref,
         dup_ref, iup_ref, vec_ref, o_ref, h_ref) = refs
        vl_ref = vf_ref = vup_ref = None
    C = CHUNK
    L = LANES

    @pl.when(pl.program_id(2) == 0)
    def _():
        h_ref[...] = jnp.zeros_like(h_ref)

    ri = lax.broadcasted_iota(jnp.int32, (L, L), 0)
    ci = lax.broadcasted_iota(jnp.int32, (L, L), 1)
    same_head = (ri < C) == (ci < C)
    rt_ = ri & (C - 1)
    ct_ = ci & (C - 1)
    hm = same_head.astype(F32)
    ones_bd = same_head.astype(BF16)
    tri_s = same_head & (ct_ < rt_)
    tri_i = same_head & (ct_ <= rt_)
    eye = ri == ci
    eye_f = eye.astype(F32)
    r64 = lax.broadcasted_iota(jnp.int32, (C, C), 0)
    c64 = lax.broadcasted_iota(jnp.int32, (C, C), 1)
    tril64 = (c64 <= r64).astype(BF16)

    vec = vec_ref[...]
    decay_bias = vec[0:1]
    iclr_bias = vec[1:2]
    vres_bias = vec[2:3]
    k_k = vec[3:4]
    k_a = vec[4:5]
    r_k = vec[5:6]
    gn_w = vec[6:7]
    gn_b = vec[7:8]
    dup = dup_ref[...]
    iup = iup_ref[...]
    vup = vup_ref[...] if has_vres else None

    def segsum(x):
        hi, lo = _split2(x)
        return _dot(hi, ones_bd) + _dot(lo, ones_bd)

    def stack(x):
        return jnp.concatenate([x, x], axis=0) * hm

    def dup2(x):
        return jnp.concatenate([x, x], axis=0)

    def body(c, carry):
        rows = pl.ds(pl.multiple_of(c * C, C), C)
        r = r_ref[0, rows, :]
        k = k_ref[0, rows, :]
        v = v_ref[0, rows, :]
        z = z_ref[0, rows, :]
        wl = wl_ref[0, rows, :]
        al = al_ref[0, rows, :]

        dx = decay_bias + _dot(jnp.tanh(wl).astype(BF16), dup)
        logw = -jnp.exp(-_softplus(-dx) - 0.5)
        a = _sigmoid(iclr_bias + _dot(al.astype(BF16), iup))
        if has_vres:
            vl = vl_ref[0, rows, :]
            vf = vf_ref[0, rows, :]
            v = v + (vf - v) * _sigmoid(vres_bias + _dot(vl.astype(BF16), vup))
        kkf = k * k_k
        kk = kkf / jnp.maximum(jnp.sqrt(segsum(kkf * kkf)), 1e-12)
        k2 = k * (1.0 + (a - 1.0) * k_a)
        b = kk * a

        lw_hi, lw_lo = _split2(logw)
        cum = _dot(tril64, lw_hi) + _dot(tril64, lw_lo)
        mid = cum[C // 2 - 1:C // 2, :]
        last = cum[C - 1:C, :]
        g_in = jnp.exp(cum - mid)
        g_ex = jnp.exp(cum - logw - mid)
        g_inv = jnp.exp(mid - cum)
        g_end = jnp.exp(last - cum)
        gm = jnp.exp(mid)
        g_c = jnp.exp(last)

        at_m = stack(-kk * g_ex)
        rt_m = stack(r * g_in)
        v_m = stack(v)
        lhs = jnp.concatenate([at_m, rt_m], axis=0).astype(BF16)
        rhs = jnp.concatenate([dup2(b * g_inv), dup2(k2 * g_inv)], axis=0).astype(BF16)
        A = _dot_nt(lhs, rhs)
        a_ab = jnp.where(tri_s, A[0:L, 0:L], 0.0)
        a_ak = jnp.where(tri_s, A[0:L, L:2 * L], 0.0)
        a_rb = jnp.where(tri_i, A[L:2 * L, 0:L], 0.0)
        a_rk = jnp.where(tri_i, A[L:2 * L, L:2 * L], 0.0)

        P = a_ab
        Tm = eye_f + a_ab
        for _ in range(5):
            Pb = P.astype(BF16)
            P = _dot(Pb, Pb)
            Tm = Tm + _dot(Tm.astype(BF16), P.astype(BF16))
        Tb = Tm.astype(BF16)

        W = _dot(Tb, (at_m * gm).astype(BF16))
        Uv = _dot(Tb, _dot(a_ak.astype(BF16), v_m.astype(BF16)).astype(BF16))
        bk = jnp.concatenate([stack(b * g_end), stack(k2 * g_end)], axis=0)
        bk_t = bk.T.astype(BF16)
        uv = jnp.concatenate([Uv, v_m], axis=0).astype(BF16)
        Wb = W.astype(BF16)
        G = _dot(bk_t, uv)
        Mm = _dot(bk_t[:, 0:L], Wb) + eye_f * g_c
        R = rt_m * gm + _dot(a_rb.astype(BF16), Wb)
        Yv = _dot(jnp.concatenate([a_rb, a_rk], axis=1).astype(BF16), uv)

        H = h_ref[...]
        HY = _dot(jnp.concatenate([Mm, R], axis=0).astype(BF16), H.astype(BF16))
        h_ref[...] = HY[0:L] + G
        Ym = HY[L:2 * L] + Yv
        y = Ym[0:C] + Ym[C:2 * C]

        mean = segsum(y) * (1.0 / HEAD_DIM)
        d = y - mean
        var = segsum(d * d) * (1.0 / HEAD_DIM)
        yn = d * lax.rsqrt(var + GN_EPS) * gn_w + gn_b
        bonus = segsum(r * k2 * r_k) * v
        out = (yn + bonus) * (z * _sigmoid(z))
        o_ref[0, rows, :] = out.astype(o_ref.dtype)
        return carry

    lax.fori_loop(0, n_chunks, body, 0)


def _wkv(proj_main, proj_lora, v_first, dup, iup, vup, vec):
    B, T, D4 = proj_main.shape
    D = D4 // 4
    npair = D // LANES
    tb = _pick_tile(T, 640, CHUNK)
    has_vres = v_first is not None
    row_blk = lambda off: pl.BlockSpec((1, tb, LANES), lambda b, p, t, off=off: (b, t, off + p))
    lora_blk = lambda j: pl.BlockSpec((1, tb, LANES), lambda b, p, t, j=j: (b, t, j))
    up_blk = pl.BlockSpec((LANES, LANES), lambda b, p, t: (0, p))
    in_specs = [row_blk(0), row_blk(npair), row_blk(2 * npair), row_blk(3 * npair),
                lora_blk(0), lora_blk(1)]
    args = [proj_main, proj_main, proj_main, proj_main, proj_lora, proj_lora]
    if has_vres:
        in_specs += [lora_blk(2), row_blk(2 * npair)]
        args += [proj_lora, v_first]
    in_specs += [up_blk, up_blk]
    args += [dup, iup]
    if has_vres:
        in_specs += [up_blk]
        args += [vup]
    in_specs += [pl.BlockSpec((8, LANES), lambda b, p, t: (0, p))]
    args += [vec]
    return pl.pallas_call(
        functools.partial(_wkv_kernel, has_vres=has_vres, n_chunks=tb // CHUNK),
        grid=(B, npair, T // tb),
        in_specs=in_specs,
        out_specs=pl.BlockSpec((1, tb, LANES), lambda b, p, t: (b, t, p)),
        out_shape=jax.ShapeDtypeStruct((B, T, D), BF16),
        scratch_shapes=[pltpu.VMEM((LANES, LANES), F32)],
        compiler_params=_cparams(3),
        name="wkv7",
    )(*args)


def _out_proj_kernel(a_ref, w_ref, h_ref, o_ref):
    o_ref[0] = h_ref[0] + _dot(a_ref[0], w_ref[...])


def _out_proj(act, w, h):
    B, T, D = h.shape
    tm = _pick_tile(T, 640, 64)
    return pl.pallas_call(
        _out_proj_kernel,
        grid=(B, T // tm),
        in_specs=[
            pl.BlockSpec((1, tm, D), lambda b, i: (b, i, 0)),
            pl.BlockSpec((D, D), lambda b, i: (0, 0)),
            pl.BlockSpec((1, tm, D), lambda b, i: (b, i, 0)),
        ],
        out_specs=pl.BlockSpec((1, tm, D), lambda b, i: (b, i, 0)),
        out_shape=jax.ShapeDtypeStruct((B, T, D), F32),
        compiler_params=_cparams(2),
        name="out_proj",
    )(act, w, h)


def _head_rms(x, gain2, ones_bd, scale):
    cols = []
    for g in range(x.shape[1] // LANES):
        xg = x[:, g * LANES:(g + 1) * LANES]
        hi, lo = _split2(xg * xg)
        ms = (_dot(hi, ones_bd) + _dot(lo, ones_bd)) * (1.0 / HEAD_DIM)
        cols.append(xg * lax.rsqrt(ms + NORM_EPS) * gain2 * scale)
    return jnp.concatenate(cols, axis=1)


def _ones_bd():
    ri = lax.broadcasted_iota(jnp.int32, (LANES, LANES), 0)
    ci = lax.broadcasted_iota(jnp.int32, (LANES, LANES), 1)
    return ((ri < HEAD_DIM) == (ci < HEAD_DIM)).astype(BF16)


def _kv_kernel(x_ref, g_ref, w_ref, fb_ref, kn_ref, tril_ref, k_ref, v_ref, c_ref, carry_ref,
               *, n_pad):
    i = pl.program_id(1)
    x = x_ref[0]
    tm, D = x.shape
    ms = jnp.mean(x * x, axis=-1, keepdims=True)
    xn = (x * lax.rsqrt(ms + NORM_EPS) * g_ref[...]).astype(BF16)
    p = _dot(xn, w_ref[...])
    k_ref[0] = _head_rms(p[:, 0:D], kn_ref[...], _ones_bd(), 1.0).astype(BF16)
    v_ref[0] = p[:, D:2 * D].astype(BF16)

    @pl.when(i == 0)
    def _():
        carry_ref[...] = jnp.zeros_like(carry_ref)

    row = lax.broadcasted_iota(jnp.int32, (tm, LANES), 0) + i * tm
    logf = -_softplus(-(p[:, 2 * D:2 * D + LANES] + fb_ref[...]))
    logf = jnp.where(row >= n_pad, logf, 0.0)
    hi, mid, lo = _split3(logf)
    tril = tril_ref[...]
    c = _dot(tril, hi) + _dot(tril, mid) + _dot(tril, lo) + carry_ref[0:1, :]
    c_ref[0] = c
    carry_ref[0:1, :] = c[tm - 1:tm, :]


def _shared_kv(h, g, w, fbias, knorm2, n_pad):
    B, T, D = h.shape
    NW = w.shape[1]
    tm = _pick_tile(T, 640, 64)
    tril = jnp.tril(jnp.ones((tm, tm), BF16))
    return pl.pallas_call(
        functools.partial(_kv_kernel, n_pad=n_pad),
        grid=(B, T // tm),
        in_specs=[
            pl.BlockSpec((1, tm, D), lambda b, i: (b, i, 0)),
            pl.BlockSpec((1, D), lambda b, i: (0, 0)),
            pl.BlockSpec((D, NW), lambda b, i: (0, 0)),
            pl.BlockSpec((1, LANES), lambda b, i: (0, 0)),
            pl.BlockSpec((1, LANES), lambda b, i: (0, 0)),
            pl.BlockSpec((tm, tm), lambda b, i: (0, 0)),
        ],
        out_specs=[
            pl.BlockSpec((1, tm, D), lambda b, i: (b, i, 0)),
            pl.BlockSpec((1, tm, D), lambda b, i: (b, i, 0)),
            pl.BlockSpec((1, tm, LANES), lambda b, i: (b, i, 0)),
        ],
        out_shape=[
            jax.ShapeDtypeStruct((B, T, D), BF16),
            jax.ShapeDtypeStruct((B, T, D), BF16),
            jax.ShapeDtypeStruct((B, T, LANES), F32),
        ],
        scratch_shapes=[pltpu.VMEM((8, LANES), F32)],
        compiler_params=_cparams(2),
        name="shared_kv",
    )(h, g.reshape(1, D), w, fbias, knorm2, tril)


def _qz_kernel(x_ref, g_ref, w_ref, qn_ref, q_ref, z_ref):
    x = x_ref[0]
    D = x.shape[1]
    ms = jnp.mean(x * x, axis=-1, keepdims=True)
    xn = (x * lax.rsqrt(ms + NORM_EPS) * g_ref[...]).astype(BF16)
    p = _dot(xn, w_ref[...])
    q_ref[0] = _head_rms(p[:, 0:D], qn_ref[...], _ones_bd(), HEAD_DIM ** -0.5).astype(BF16)
    z_ref[0] = p[:, D:2 * D]


def _qz_proj(h, g, w, qnorm2):
    B, S, D = h.shape
    tm = _pick_tile(S, 512, 64)
    return pl.pallas_call(
        _qz_kernel,
        grid=(B, S // tm),
        in_specs=[
            pl.BlockSpec((1, tm, D), lambda b, i: (b, i, 0)),
            pl.BlockSpec((1, D), lambda b, i: (0, 0)),
            pl.BlockSpec((D, 2 * D), lambda b, i: (0, 0)),
            pl.BlockSpec((1, LANES), lambda b, i: (0, 0)),
        ],
        out_specs=[
            pl.BlockSpec((1, tm, D), lambda b, i: (b, i, 0)),
            pl.BlockSpec((1, tm, D), lambda b, i: (b, i, 0)),
        ],
        out_shape=[
            jax.ShapeDtypeStruct((B, S, D), BF16),
            jax.ShapeDtypeStruct((B, S, D), F32),
        ],
        compiler_params=_cparams(2),
        name="qz_proj",
    )(h, g.reshape(1, D), w, qnorm2)


def _fox_kernel(q_ref, cq_ref, z_ref, km_ref, vm_ref, ckm_ref, k_ref, v_ref, ck_ref, o_ref,
                *, tq, n_pad):
    qi = pl.program_id(2)
    q = q_ref[0]
    lane = lax.broadcasted_iota(jnp.int32, (1, LANES), 1)
    col_m = lax.broadcasted_iota(jnp.int32, (1, FRONT), 1)
    rowq = lax.broadcasted_iota(jnp.int32, (tq, tq), 0)
    colq = lax.broadcasted_iota(jnp.int32, (tq, tq), 1)
    neg = -jnp.inf

    def online(carry, s, vs):
        m, l, acc = carry
        m_new = jnp.maximum(m, jnp.max(s, axis=-1, keepdims=True))
        alpha = jnp.exp(m - m_new)
        p = jnp.exp(s - m_new)
        l = l * alpha + jnp.sum(p, axis=-1, keepdims=True)
        acc = acc * alpha + _dot(p.astype(BF16), vs)
        return m_new, l, acc

    outs = []
    for hh in range(2):
        head_lanes = (lane < HEAD_DIM) if hh == 0 else (lane >= HEAD_DIM)
        qm = jnp.where(head_lanes, q, jnp.zeros_like(q))
        cq = cq_ref[0, hh]

        s = _dot_nt(qm, km_ref[0]) + cq - ckm_ref[0, hh]
        s = jnp.where(col_m >= n_pad, s, neg)
        m = jnp.max(s, axis=-1, keepdims=True)
        p = jnp.exp(s - m)
        carry = (m, jnp.sum(p, axis=-1, keepdims=True), _dot(p.astype(BF16), vm_ref[0]))

        def body(kb, carry):
            rows = pl.ds(pl.multiple_of(kb * tq, tq), tq)
            s = _dot_nt(qm, k_ref[0, rows, :]) + cq - ck_ref[0, kb, hh:hh + 1, :]
            return online(carry, s, v_ref[0, rows, :])

        carry = lax.fori_loop(0, qi, body, carry)

        rows = pl.ds(pl.multiple_of(qi * tq, tq), tq)
        s = _dot_nt(qm, k_ref[0, rows, :]) + cq - ck_ref[0, qi, hh:hh + 1, :]
        s = jnp.where(colq <= rowq, s, neg)
        m, l, acc = online(carry, s, v_ref[0, rows, :])
        outs.append(acc / l)

    z = z_ref[0]
    out = jnp.where(lane < HEAD_DIM, outs[0], outs[1]) * (z * _sigmoid(z))
    o_ref[0] = out.astype(o_ref.dtype)


def _fox_attention(q, cq, z, k_meta, v_meta, ck_meta, k_real, v_real, ck_real, n_pad):
    B, S, D = q.shape
    npair = D // LANES
    tq = _pick_tile(S, 512, 128)
    nq = S // tq
    return pl.pallas_call(
        functools.partial(_fox_kernel, tq=tq, n_pad=n_pad),
        grid=(B, npair, nq),
        in_specs=[
            pl.BlockSpec((1, tq, LANES), lambda b, p, i: (b, i, p)),
            pl.BlockSpec((1, 2, tq, 1), lambda b, p, i: (b * npair + p, 0, i, 0)),
            pl.BlockSpec((1, tq, LANES), lambda b, p, i: (b, i, p)),
            pl.BlockSpec((1, FRONT, LANES), lambda b, p, i: (b, 0, p)),
            pl.BlockSpec((1, FRONT, LANES), lambda b, p, i: (b, 0, p)),
            pl.BlockSpec((1, 2, 1, FRONT), lambda b, p, i: (b * npair + p, 0, 0, 0)),
            pl.BlockSpec((1, S, LANES), lambda b, p, i: (b, 0, p)),
            pl.BlockSpec((1, S, LANES), lambda b, p, i: (b, 0, p)),
            pl.BlockSpec((1, nq, 2, tq), lambda b, p, i: (b * npair + p, 0, 0, 0)),
        ],
        out_specs=pl.BlockSpec((1, tq, LANES), lambda b, p, i: (b, i, p)),
        out_shape=jax.ShapeDtypeStruct((B, S, D), BF16),
        compiler_params=_cparams(3),
        name="fox_attention",
    )(q, cq, z, k_meta, v_meta, ck_meta, k_real, v_real, ck_real)


def _pad_rows(m, rows):
    return jnp.pad(m, ((0, rows - m.shape[0]), (0, 0)))


def kernel(x, meta_tokens, a_norm, a_w_in, a_shift_mu, a_vres_down, a_vres_mu, a_vres_up, a_vres_bias, a_decay_up, a_decay_bias, a_iclr_up, a_iclr_bias, a_k_k, a_k_a, a_r_k, a_gn_w, a_gn_b, a_w_out, kv_norm, kv_w, kv_f_bias, k_norm, b_norm, b_w_in, b_q_norm, b_w_out):
    B, S, D = x.shape
    H = D // HEAD_DIM
    npair = D // LANES
    n_a = a_w_in.shape[0]
    n_b = b_w_in.shape[0]
    n_pad = FRONT - N_META
    lora_d = a_decay_up.shape[1]
    lora_i = a_iclr_up.shape[1]

    meta = jnp.broadcast_to(meta_tokens.astype(x.dtype)[None], (B, N_META, D))
    h = jnp.concatenate([jnp.zeros((B, n_pad, D), x.dtype), meta, x], axis=1)

    v_first = None
    for l in range(n_a):
        w_in = a_w_in[l]
        mu = a_shift_mu[l]
        w_main = w_in[:, :4 * D].astype(BF16)
        w_lora = jnp.zeros((D, 3 * LANES), F32)
        mu_lora = jnp.zeros((3 * LANES,), F32)
        w_lora = w_lora.at[:, 0:lora_d].set(w_in[:, 4 * D:4 * D + lora_d])
        mu_lora = mu_lora.at[0:lora_d].set(mu[4 * D:4 * D + lora_d])
        w_lora = w_lora.at[:, LANES:LANES + lora_i].set(w_in[:, 4 * D + lora_d:4 * D + lora_d + lora_i])
        mu_lora = mu_lora.at[LANES:LANES + lora_i].set(mu[4 * D + lora_d:4 * D + lora_d + lora_i])
        vup = None
        vres_bias = jnp.zeros((D,), F32)
        if l > 0:
            lora_v = a_vres_down.shape[2]
            w_lora = w_lora.at[:, 2 * LANES:2 * LANES + lora_v].set(a_vres_down[l - 1])
            mu_lora = mu_lora.at[2 * LANES:2 * LANES + lora_v].set(a_vres_mu[l - 1])
            vup = _pad_rows(a_vres_up[l - 1], LANES).astype(BF16)
            vres_bias = a_vres_bias[l - 1]
        proj_main = _proj_shift(h, a_norm[l], w_main, mu[:4 * D], 2 * D)
        proj_lora = _proj_shift(h, a_norm[l], w_lora.astype(BF16), mu_lora, 3 * LANES)
        vec = jnp.stack([a_decay_bias[l], a_iclr_bias[l], vres_bias, a_k_k[l], a_k_a[l],
                         a_r_k[l], a_gn_w[l], a_gn_b[l]], axis=0)
        gated = _wkv(proj_main, proj_lora, v_first if l > 0 else None,
                     _pad_rows(a_decay_up[l], LANES).astype(BF16),
                     _pad_rows(a_iclr_up[l], LANES).astype(BF16), vup, vec)
        if l == 0:
            v_first = proj_main
        h = _out_proj(gated, a_w_out[l].astype(BF16), h)

    w_kv = jnp.zeros((D, 2 * D + LANES), F32).at[:, :2 * D + H].set(kv_w).astype(BF16)
    fbias = jnp.zeros((1, LANES), F32).at[0, :H].set(kv_f_bias)
    knorm2 = jnp.concatenate([k_norm, k_norm]).reshape(1, LANES)
    k_sh, v_sh, c_all = _shared_kv(h, kv_norm, w_kv, fbias, knorm2, n_pad)

    tq = _pick_tile(S, 512, 128)
    nq = S // tq
    c_t = jnp.swapaxes(c_all[:, :, :H], 1, 2)
    cq = c_t[:, :, FRONT:].reshape(B * npair, 2, S, 1)
    ck_real = c_t[:, :, FRONT:].reshape(B * npair, 2, nq, tq).swapaxes(1, 2)
    ck_meta = c_t[:, :, :FRONT].reshape(B * npair, 2, 1, FRONT)
    k_meta, k_real = k_sh[:, :FRONT], k_sh[:, FRONT:]
    v_meta, v_real = v_sh[:, :FRONT], v_sh[:, FRONT:]

    h = h[:, FRONT:]
    for j in range(n_b):
        qnorm2 = jnp.concatenate([b_q_norm[j], b_q_norm[j]]).reshape(1, LANES)
        q, z = _qz_proj(h, b_norm[j], b_w_in[j].astype(BF16), qnorm2)
        attn = _fox_attention(q, cq, z, k_meta, v_meta, ck_meta, k_real, v_real, ck_real, n_pad)
        h = _out_proj(attn, b_w_out[j].astype(BF16), h)
    return h
```

```python
import functools

import jax
import jax.numpy as jnp
from jax import lax
from jax.experimental import pallas as pl
from jax.experimental.pallas import tpu as pltpu

N_META = 16
HEAD_DIM = 64
NORM_EPS = 1e-6
GN_EPS = 64e-5
LANES = 128
CHUNK = 64
WKV_UNROLL = 10
WKV_SKEW = 1
FRONT = 128
VMEM_LIMIT = 56 * 1024 * 1024

F32 = jnp.float32
BF16 = jnp.bfloat16
_NT = (((1,), (1,)), ((), ()))


def _dot(a, b):
    return jnp.dot(a, b, preferred_element_type=F32)


def _dot_nt(a, b):
    return lax.dot_general(a, b, _NT, preferred_element_type=F32)


def _split2(x):
    hi = x.astype(BF16)
    lo = (x - hi.astype(F32)).astype(BF16)
    return hi, lo


def _split3(x):
    hi = x.astype(BF16)
    r1 = x - hi.astype(F32)
    mid = r1.astype(BF16)
    lo = (r1 - mid.astype(F32)).astype(BF16)
    return hi, mid, lo


def _sigmoid(x):
    return 1.0 / (1.0 + jnp.exp(-x))


def _softplus(x):
    return jnp.maximum(x, 0.0) + jnp.log(1.0 + jnp.exp(-jnp.abs(x)))


def _pick_tile(n, cap, mult):
    best = mult
    for t in range(mult, cap + 1, mult):
        if n % t == 0:
            best = t
    return best


def _cparams(n_axes):
    return pltpu.CompilerParams(dimension_semantics=("arbitrary",) * n_axes,
                                vmem_limit_bytes=VMEM_LIMIT)


def _proj_shift_kernel(x_ref, g_ref, w_ref, mu_ref, o_ref, carry_ref):
    i = pl.program_id(2)
    x = x_ref[0]
    ms = jnp.mean(x * x, axis=-1, keepdims=True)
    xn = (x * lax.rsqrt(ms + NORM_EPS) * g_ref[...]).astype(BF16)
    p = _dot(xn, w_ref[...])
    tm = p.shape[0]

    @pl.when(i == 0)
    def _():
        carry_ref[...] = jnp.zeros_like(carry_ref)

    prev = pltpu.roll(p, 1, axis=0)
    row = lax.broadcasted_iota(jnp.int32, p.shape, 0)
    prev = jnp.where(row == 0, carry_ref[0:1, :], prev)
    o_ref[0] = p + mu_ref[...] * (prev - p)
    carry_ref[0:1, :] = p[tm - 1:tm, :]


def _proj_shift(h, g, w, mu, tn):
    B, T, D = h.shape
    NW = w.shape[1]
    tm = _pick_tile(T, 640, 64)
    return pl.pallas_call(
        _proj_shift_kernel,
        grid=(NW // tn, B, T // tm),
        in_specs=[
            pl.BlockSpec((1, tm, D), lambda j, b, i: (b, i, 0)),
            pl.BlockSpec((1, D), lambda j, b, i: (0, 0)),
            pl.BlockSpec((D, tn), lambda j, b, i: (0, j)),
            pl.BlockSpec((1, tn), lambda j, b, i: (0, j)),
        ],
        out_specs=pl.BlockSpec((1, tm, tn), lambda j, b, i: (b, i, j)),
        out_shape=jax.ShapeDtypeStruct((B, T, NW), F32),
        scratch_shapes=[pltpu.VMEM((8, tn), F32)],
        compiler_params=_cparams(3),
        name="proj_shift",
    )(h, g.reshape(1, D), w, mu.reshape(1, NW))


def _wkv_kernel(*refs, has_vres, n_chunks, unroll, skew):
    if has_vres:
        (r_ref, k_ref, v_ref, z_ref, wl_ref, al_ref, vl_ref, vf_ref,
         dup_ref, iup_ref, vup_ref, vec_ref, o_ref, h_ref) = refs
    else:
        (r_ref, k_ref, v_ref, z_ref, wl_ref, al_ref,
         dup_ref, iup_ref, vec_ref, o_ref, h_ref) = refs
        vl_ref = vf_ref = vup_ref = None
    C = CHUNK
    L = LANES

    @pl.when(pl.program_id(2) == 0)
    def _():
        h_ref[...] = jnp.zeros_like(h_ref)

    ri = lax.broadcasted_iota(jnp.int32, (L, L), 0)
    ci = lax.broadcasted_iota(jnp.int32, (L, L), 1)
    same_head = (ri < C) == (ci < C)
    rt_ = ri & (C - 1)
    ct_ = ci & (C - 1)
    hm = same_head.astype(F32)
    tri_s = same_head & (ct_ < rt_)
    tri_i = same_head & (ct_ <= rt_)
    eye = ri == ci
    eye_f = eye.astype(F32)
    r64 = lax.broadcasted_iota(jnp.int32, (C, C), 0)
    c64 = lax.broadcasted_iota(jnp.int32, (C, C), 1)
    tril64 = (c64 <= r64).astype(BF16)

    vec = vec_ref[...]
    decay_bias = vec[0:1]
    iclr_bias = vec[1:2]
    vres_bias = vec[2:3]
    k_k = vec[3:4]
    k_a = vec[4:5]
    r_k = vec[5:6]
    gn_w = vec[6:7]
    gn_b = vec[7:8]
    dup = dup_ref[...]
    iup = iup_ref[...]
    vup = vup_ref[...] if has_vres else None

    lane_h0 = lax.broadcasted_iota(jnp.int32, (C, L), 1) < HEAD_DIM

    def segsum(x):
        s0 = jnp.sum(jnp.where(lane_h0, x, 0.0), axis=-1, keepdims=True)
        s1 = jnp.sum(jnp.where(lane_h0, 0.0, x), axis=-1, keepdims=True)
        return jnp.where(lane_h0, s0, s1)

    def stack(x):
        return jnp.concatenate([x, x], axis=0) * hm

    def dup2(x):
        return jnp.concatenate([x, x], axis=0)

    zeros_ll = jnp.zeros((L, L), BF16)

    def chunk_stages(c):
        rows = pl.ds(pl.multiple_of(c * C, C), C)
        r = r_ref[0, rows, :]
        k = k_ref[0, rows, :]
        v = v_ref[0, rows, :]
        wl = wl_ref[0, rows, :]
        al = al_ref[0, rows, :]

        dx = decay_bias + _dot(jnp.tanh(wl).astype(BF16), dup)
        logw = -jnp.exp(-_softplus(-dx) - 0.5)
        a = _sigmoid(iclr_bias + _dot(al.astype(BF16), iup))
        if has_vres:
            vl = vl_ref[0, rows, :]
            vf = vf_ref[0, rows, :]
            v = v + (vf - v) * _sigmoid(vres_bias + _dot(vl.astype(BF16), vup))
        kkf = k * k_k
        yield
        kk = kkf / jnp.maximum(jnp.sqrt(segsum(kkf * kkf)), 1e-12)
        k2 = k * (1.0 + (a - 1.0) * k_a)
        b = kk * a

        lw_hi, lw_lo = _split2(logw)
        cum = _dot(tril64, lw_hi) + _dot(tril64, lw_lo)
        yield
        mid = cum[C // 2 - 1:C // 2, :]
        last = cum[C - 1:C, :]
        g_in = jnp.exp(cum - mid)
        g_ex = jnp.exp(cum - logw - mid)
        g_inv = jnp.exp(mid - cum)
        g_end = jnp.exp(last - cum)
        gm = jnp.exp(mid)
        g_c = jnp.exp(last)

        at_m = stack(-kk * g_ex)
        rt_m = stack(r * g_in)
        v_mb = stack(v).astype(BF16)
        lhs = jnp.concatenate([at_m, rt_m], axis=0).astype(BF16)
        rhs = jnp.concatenate([dup2(b * g_inv), dup2(k2 * g_inv)], axis=0).astype(BF16)
        A = _dot_nt(lhs, rhs)
        yield
        a_ab = jnp.where(tri_s, A[0:L, 0:L], 0.0)
        a_ak = jnp.where(tri_s, A[0:L, L:2 * L], 0.0).astype(BF16)
        a_rb = jnp.where(tri_i, A[L:2 * L, 0:L], 0.0).astype(BF16)
        a_rk = jnp.where(tri_i, A[L:2 * L, L:2 * L], 0.0).astype(BF16)
        AV = _dot(a_ak, v_mb)

        Pb = a_ab.astype(BF16)
        Sm = eye_f + a_ab
        Pb = _dot(Pb, Pb).astype(BF16)
        yield
        for _ in range(4):
            PS = _dot(Pb, jnp.concatenate([Pb, Sm.astype(BF16)], axis=1))
            yield
            Pb = PS[:, 0:L].astype(BF16)
            Sm = Sm + PS[:, L:2 * L]
        Sm = Sm + _dot(Pb, Sm.astype(BF16))
        yield

        X = _dot(Sm.astype(BF16),
                 jnp.concatenate([(at_m * gm).astype(BF16), AV.astype(BF16)], axis=1))
        bk = jnp.concatenate([stack(b * g_end), stack(k2 * g_end)], axis=0)
        lhs_big = jnp.concatenate(
            [jnp.concatenate([a_rb, a_rk], axis=1), bk.T.astype(BF16)], axis=0)
        yield
        rhs_big = jnp.concatenate(
            [X.astype(BF16), jnp.concatenate([zeros_ll, v_mb], axis=1)], axis=0)
        big = _dot(lhs_big, rhs_big)
        yield
        rm = big[:, 0:L] + jnp.concatenate([rt_m * gm, eye_f * g_c], axis=0)
        bonus = segsum(r * k2 * r_k) * v
        yield (rows, rm.astype(BF16), big, bonus)

    def epilogue_stages(rows, y, bonus):
        mean = segsum(y) * (1.0 / HEAD_DIM)
        yield
        d = y - mean
        var = segsum(d * d) * (1.0 / HEAD_DIM)
        yield
        yn = d * lax.rsqrt(var + GN_EPS) * gn_w + gn_b
        z = z_ref[0, rows, :]
        out = (yn + bonus) * (z * _sigmoid(z))
        o_ref[0, rows, :] = out.astype(o_ref.dtype)
        yield True

    def full_chunk(c, state):
        gen = chunk_stages(c)
        out = next(gen)
        while out is None:
            yield
            out = next(gen)
        rows, rm_b, big, bonus = out
        Hb = state["H"].astype(BF16)
        Hn = _dot(rm_b[L:2 * L], Hb) + big[L:2 * L, L:2 * L]
        state["H"] = Hn
        yield
        Ym = _dot(rm_b[0:L], Hb) + big[0:L, L:2 * L]
        yield
        yield from epilogue_stages(rows, Ym[0:C] + Ym[C:2 * C], bonus)

    def body(i, carry):
        state = {"H": h_ref[...]}
        gens = [full_chunk(i * unroll + u, state) for u in range(unroll)]
        done = [False] * unroll
        t = 0
        while not all(done):
            for u, g in enumerate(gens):
                if not done[u] and t >= skew * u:
                    done[u] = next(g) is True
            t += 1
        h_ref[...] = state["H"]
        return carry

    lax.fori_loop(0, n_chunks // unroll, body, 0)


def _wkv(proj_main, proj_lora, v_first, dup, iup, vup, vec):
    B, T, D4 = proj_main.shape
    D = D4 // 4
    npair = D // LANES
    tb = _pick_tile(T, 640, CHUNK)
    has_vres = v_first is not None
    row_blk = lambda off: pl.BlockSpec((1, tb, LANES), lambda b, p, t, off=off: (b, t, off + p))
    lora_blk = lambda j: pl.BlockSpec((1, tb, LANES), lambda b, p, t, j=j: (b, t, j))
    up_blk = pl.BlockSpec((LANES, LANES), lambda b, p, t: (0, p))
    in_specs = [row_blk(0), row_blk(npair), row_blk(2 * npair), row_blk(3 * npair),
                lora_blk(0), lora_blk(1)]
    args = [proj_main, proj_main, proj_main, proj_main, proj_lora, proj_lora]
    if has_vres:
        in_specs += [lora_blk(2), row_blk(2 * npair)]
        args += [proj_lora, v_first]
    in_specs += [up_blk, up_blk]
    args += [dup, iup]
    if has_vres:
        in_specs += [up_blk]
        args += [vup]
    in_specs += [pl.BlockSpec((8, LANES), lambda b, p, t: (0, p))]
    args += [vec]
    return pl.pallas_call(
        functools.partial(_wkv_kernel, has_vres=has_vres, n_chunks=tb // CHUNK,
                          unroll=WKV_UNROLL, skew=WKV_SKEW),
        grid=(B, npair, T // tb),
        in_specs=in_specs,
        out_specs=pl.BlockSpec((1, tb, LANES), lambda b, p, t: (b, t, p)),
        out_shape=jax.ShapeDtypeStruct((B, T, D), BF16),
        scratch_shapes=[pltpu.VMEM((LANES, LANES), F32)],
        compiler_params=_cparams(3),
        name="wkv7",
    )(*args)


def _out_proj_kernel(a_ref, w_ref, h_ref, o_ref):
    o_ref[0] = h_ref[0] + _dot(a_ref[0], w_ref[...])


def _out_proj(act, w, h):
    B, T, D = h.shape
    tm = _pick_tile(T, 640, 64)
    return pl.pallas_call(
        _out_proj_kernel,
        grid=(B, T // tm),
        in_specs=[
            pl.BlockSpec((1, tm, D), lambda b, i: (b, i, 0)),
            pl.BlockSpec((D, D), lambda b, i: (0, 0)),
            pl.BlockSpec((1, tm, D), lambda b, i: (b, i, 0)),
        ],
        out_specs=pl.BlockSpec((1, tm, D), lambda b, i: (b, i, 0)),
        out_shape=jax.ShapeDtypeStruct((B, T, D), F32),
        compiler_params=_cparams(2),
        name="out_proj",
    )(act, w, h)


def _head_rms(x, gain2, ones_bd, scale):
    cols = []
    for g in range(x.shape[1] // LANES):
        xg = x[:, g * LANES:(g + 1) * LANES]
        hi, lo = _split2(xg * xg)
        ms = (_dot(hi, ones_bd) + _dot(lo, ones_bd)) * (1.0 / HEAD_DIM)
        cols.append(xg * lax.rsqrt(ms + NORM_EPS) * gain2 * scale)
    return jnp.concatenate(cols, axis=1)


def _ones_bd():
    ri = lax.broadcasted_iota(jnp.int32, (LANES, LANES), 0)
    ci = lax.broadcasted_iota(jnp.int32, (LANES, LANES), 1)
    return ((ri < HEAD_DIM) == (ci < HEAD_DIM)).astype(BF16)


def _kv_kernel(x_ref, g_ref, w_ref, fb_ref, kn_ref, tril_ref, k_ref, v_ref, c_ref, carry_ref,
               *, n_pad):
    i = pl.program_id(1)
    x = x_ref[0]
    tm, D = x.shape
    ms = jnp.mean(x * x, axis=-1, keepdims=True)
    xn = (x * lax.rsqrt(ms + NORM_EPS) * g_ref[...]).astype(BF16)
    p = _dot(xn, w_ref[...])
    k_ref[0] = _head_rms(p[:, 0:D], kn_ref[...], _ones_bd(), 1.0).astype(BF16)
    v_ref[0] = p[:, D:2 * D].astype(BF16)

    @pl.when(i == 0)
    def _():
        carry_ref[...] = jnp.zeros_like(carry_ref)

    row = lax.broadcasted_iota(jnp.int32, (tm, LANES), 0) + i * tm
    logf = -_softplus(-(p[:, 2 * D:2 * D + LANES] + fb_ref[...]))
    logf = jnp.where(row >= n_pad, logf, 0.0)
    hi, mid, lo = _split3(logf)
    tril = tril_ref[...]
    c = _dot(tril, hi) + _dot(tril, mid) + _dot(tril, lo) + carry_ref[0:1, :]
    c_ref[0] = c
    carry_ref[0:1, :] = c[tm - 1:tm, :]


def _shared_kv(h, g, w, fbias, knorm2, n_pad):
    B, T, D = h.shape
    NW = w.shape[1]
    tm = _pick_tile(T, 640, 64)
    tril = jnp.tril(jnp.ones((tm, tm), BF16))
    return pl.pallas_call(
        functools.partial(_kv_kernel, n_pad=n_pad),
        grid=(B, T // tm),
        in_specs=[
            pl.BlockSpec((1, tm, D), lambda b, i: (b, i, 0)),
            pl.BlockSpec((1, D), lambda b, i: (0, 0)),
            pl.BlockSpec((D, NW), lambda b, i: (0, 0)),
            pl.BlockSpec((1, LANES), lambda b, i: (0, 0)),
            pl.BlockSpec((1, LANES), lambda b, i: (0, 0)),
            pl.BlockSpec((tm, tm), lambda b, i: (0, 0)),
        ],
        out_specs=[
            pl.BlockSpec((1, tm, D), lambda b, i: (b, i, 0)),
            pl.BlockSpec((1, tm, D), lambda b, i: (b, i, 0)),
            pl.BlockSpec((1, tm, LANES), lambda b, i: (b, i, 0)),
        ],
        out_shape=[
            jax.ShapeDtypeStruct((B, T, D), BF16),
            jax.ShapeDtypeStruct((B, T, D), BF16),
            jax.ShapeDtypeStruct((B, T, LANES), F32),
        ],
        scratch_shapes=[pltpu.VMEM((8, LANES), F32)],
        compiler_params=_cparams(2),
        name="shared_kv",
    )(h, g.reshape(1, D), w, fbias, knorm2, tril)


def _qz_kernel(x_ref, g_ref, w_ref, qn_ref, q_ref, z_ref):
    x = x_ref[0]
    D = x.shape[1]
    ms = jnp.mean(x * x, axis=-1, keepdims=True)
    xn = (x * lax.rsqrt(ms + NORM_EPS) * g_ref[...]).astype(BF16)
    p = _dot(xn, w_ref[...])
    q_ref[0] = _head_rms(p[:, 0:D], qn_ref[...], _ones_bd(), HEAD_DIM ** -0.5).astype(BF16)
    z_ref[0] = p[:, D:2 * D]


def _qz_proj(h, g, w, qnorm2):
    B, S, D = h.shape
    tm = _pick_tile(S, 512, 64)
    return pl.pallas_call(
        _qz_kernel,
        grid=(B, S // tm),
        in_specs=[
            pl.BlockSpec((1, tm, D), lambda b, i: (b, i, 0)),
            pl.BlockSpec((1, D), lambda b, i: (0, 0)),
            pl.BlockSpec((D, 2 * D), lambda b, i: (0, 0)),
            pl.BlockSpec((1, LANES), lambda b, i: (0, 0)),
        ],
        out_specs=[
            pl.BlockSpec((1, tm, D), lambda b, i: (b, i, 0)),
            pl.BlockSpec((1, tm, D), lambda b, i: (b, i, 0)),
        ],
        out_shape=[
            jax.ShapeDtypeStruct((B, S, D), BF16),
            jax.ShapeDtypeStruct((B, S, D), F32),
        ],
        compiler_params=_cparams(2),
        name="qz_proj",
    )(h, g.reshape(1, D), w, qnorm2)


def _fox_kernel(q_ref, cq_ref, z_ref, km_ref, vm_ref, ckm_ref, k_ref, v_ref, ck_ref, o_ref,
                *, tq, n_pad):
    qi = pl.program_id(2)
    q = q_ref[0]
    lane = lax.broadcasted_iota(jnp.int32, (1, LANES), 1)
    col_m = lax.broadcasted_iota(jnp.int32, (1, FRONT), 1)
    rowq = lax.broadcasted_iota(jnp.int32, (tq, tq), 0)
    colq = lax.broadcasted_iota(jnp.int32, (tq, tq), 1)
    neg = -jnp.inf

    def online(carry, s, vs):
        m, l, acc = carry
        m_new = jnp.maximum(m, jnp.max(s, axis=-1, keepdims=True))
        alpha = jnp.exp(m - m_new)
        p = jnp.exp(s - m_new)
        l = l * alpha + jnp.sum(p, axis=-1, keepdims=True)
        acc = acc * alpha + _dot(p.astype(BF16), vs)
        return m_new, l, acc

    outs = []
    for hh in range(2):
        head_lanes = (lane < HEAD_DIM) if hh == 0 else (lane >= HEAD_DIM)
        qm = jnp.where(head_lanes, q, jnp.zeros_like(q))
        cq = cq_ref[0, hh]

        s = _dot_nt(qm, km_ref[0]) + cq - ckm_ref[0, hh]
        s = jnp.where(col_m >= n_pad, s, neg)
        m = jnp.max(s, axis=-1, keepdims=True)
        p = jnp.exp(s - m)
        carry = (m, jnp.sum(p, axis=-1, keepdims=True), _dot(p.astype(BF16), vm_ref[0]))

        def body(kb, carry):
            rows = pl.ds(pl.multiple_of(kb * tq, tq), tq)
            s = _dot_nt(qm, k_ref[0, rows, :]) + cq - ck_ref[0, kb, hh:hh + 1, :]
            return online(carry, s, v_ref[0, rows, :])

        carry = lax.fori_loop(0, qi, body, carry)

        rows = pl.ds(pl.multiple_of(qi * tq, tq), tq)
        s = _dot_nt(qm, k_ref[0, rows, :]) + cq - ck_ref[0, qi, hh:hh + 1, :]
        s = jnp.where(colq <= rowq, s, neg)
        m, l, acc = online(carry, s, v_ref[0, rows, :])
        outs.append(acc / l)

    z = z_ref[0]
    out = jnp.where(lane < HEAD_DIM, outs[0], outs[1]) * (z * _sigmoid(z))
    o_ref[0] = out.astype(o_ref.dtype)


def _fox_attention(q, cq, z, k_meta, v_meta, ck_meta, k_real, v_real, ck_real, n_pad):
    B, S, D = q.shape
    npair = D // LANES
    tq = _pick_tile(S, 512, 128)
    nq = S // tq
    return pl.pallas_call(
        functools.partial(_fox_kernel, tq=tq, n_pad=n_pad),
        grid=(B, npair, nq),
        in_specs=[
            pl.BlockSpec((1, tq, LANES), lambda b, p, i: (b, i, p)),
            pl.BlockSpec((1, 2, tq, 1), lambda b, p, i: (b * npair + p, 0, i, 0)),
            pl.BlockSpec((1, tq, LANES), lambda b, p, i: (b, i, p)),
            pl.BlockSpec((1, FRONT, LANES), lambda b, p, i: (b, 0, p)),
            pl.BlockSpec((1, FRONT, LANES), lambda b, p, i: (b, 0, p)),
            pl.BlockSpec((1, 2, 1, FRONT), lambda b, p, i: (b * npair + p, 0, 0, 0)),
            pl.BlockSpec((1, S, LANES), lambda b, p, i: (b, 0, p)),
            pl.BlockSpec((1, S, LANES), lambda b, p, i: (b, 0, p)),
            pl.BlockSpec((1, nq, 2, tq), lambda b, p, i: (b * npair + p, 0, 0, 0)),
        ],
        out_specs=pl.BlockSpec((1, tq, LANES), lambda b, p, i: (b, i, p)),
        out_shape=jax.ShapeDtypeStruct((B, S, D), BF16),
        compiler_params=_cparams(3),
        name="fox_attention",
    )(q, cq, z, k_meta, v_meta, ck_meta, k_real, v_real, ck_real)


def _pad_rows(m, rows):
    return jnp.pad(m, ((0, rows - m.shape[0]), (0, 0)))


def kernel(x, meta_tokens, a_norm, a_w_in, a_shift_mu, a_vres_down, a_vres_mu, a_vres_up, a_vres_bias, a_decay_up, a_decay_bias, a_iclr_up, a_iclr_bias, a_k_k, a_k_a, a_r_k, a_gn_w, a_gn_b, a_w_out, kv_norm, kv_w, kv_f_bias, k_norm, b_norm, b_w_in, b_q_norm, b_w_out):
    B, S, D = x.shape
    H = D // HEAD_DIM
    npair = D // LANES
    n_a = a_w_in.shape[0]
    n_b = b_w_in.shape[0]
    n_pad = FRONT - N_META
    lora_d = a_decay_up.shape[1]
    lora_i = a_iclr_up.shape[1]

    meta = jnp.broadcast_to(meta_tokens.astype(x.dtype)[None], (B, N_META, D))
    h = jnp.concatenate([jnp.zeros((B, n_pad, D), x.dtype), meta, x], axis=1)

    v_first = None
    for l in range(n_a):
        w_in = a_w_in[l]
        mu = a_shift_mu[l]
        w_main = w_in[:, :4 * D].astype(BF16)
        w_lora = jnp.zeros((D, 3 * LANES), F32)
        mu_lora = jnp.zeros((3 * LANES,), F32)
        w_lora = w_lora.at[:, 0:lora_d].set(w_in[:, 4 * D:4 * D + lora_d])
        mu_lora = mu_lora.at[0:lora_d].set(mu[4 * D:4 * D + lora_d])
        w_lora = w_lora.at[:, LANES:LANES + lora_i].set(w_in[:, 4 * D + lora_d:4 * D + lora_d + lora_i])
        mu_lora = mu_lora.at[LANES:LANES + lora_i].set(mu[4 * D + lora_d:4 * D + lora_d + lora_i])
        vup = None
        vres_bias = jnp.zeros((D,), F32)
        if l > 0:
            lora_v = a_vres_down.shape[2]
            w_lora = w_lora.at[:, 2 * LANES:2 * LANES + lora_v].set(a_vres_down[l - 1])
            mu_lora = mu_lora.at[2 * LANES:2 * LANES + lora_v].set(a_vres_mu[l - 1])
            vup = _pad_rows(a_vres_up[l - 1], LANES).astype(BF16)
            vres_bias = a_vres_bias[l - 1]
        proj_main = _proj_shift(h, a_norm[l], w_main, mu[:4 * D], 2 * D)
        proj_lora = _proj_shift(h, a_norm[l], w_lora.astype(BF16), mu_lora, 3 * LANES)
        vec = jnp.stack([a_decay_bias[l], a_iclr_bias[l], vres_bias, a_k_k[l], a_k_a[l],
                         a_r_k[l], a_gn_w[l], a_gn_b[l]], axis=0)
        gated = _wkv(proj_main, proj_lora, v_first if l > 0 else None,
                     _pad_rows(a_decay_up[l], LANES).astype(BF16),
                     _pad_rows(a_iclr_up[l], LANES).astype(BF16), vup, vec)
        if l == 0:
            v_first = proj_main
        h = _out_proj(gated, a_w_out[l].astype(BF16), h)

    w_kv = jnp.zeros((D, 2 * D + LANES), F32).at[:, :2 * D + H].set(kv_w).astype(BF16)
    fbias = jnp.zeros((1, LANES), F32).at[0, :H].set(kv_f_bias)
    knorm2 = jnp.concatenate([k_norm, k_norm]).reshape(1, LANES)
    k_sh, v_sh, c_all = _shared_kv(h, kv_norm, w_kv, fbias, knorm2, n_pad)

    tq = _pick_tile(S, 512, 128)
    nq = S // tq
    c_t = jnp.swapaxes(c_all[:, :, :H], 1, 2)
    cq = c_t[:, :, FRONT:].reshape(B * npair, 2, S, 1)
    ck_real = c_t[:, :, FRONT:].reshape(B * npair, 2, nq, tq).swapaxes(1, 2)
    ck_meta = c_t[:, :, :FRONT].reshape(B * npair, 2, 1, FRONT)
    k_meta, k_real = k_sh[:, :FRONT], k_sh[:, FRONT:]
    v_meta, v_real = v_sh[:, :FRONT], v_sh[:, FRONT:]

    h = h[:, FRONT:]
    for j in range(n_b):
        qnorm2 = jnp.concatenate([b_q_norm[j], b_q_norm[j]]).reshape(1, LANES)
        q, z = _qz_proj(h, b_norm[j], b_w_in[j].astype(BF16), qnorm2)
        attn = _fox_attention(q, cq, z, k_meta, v_meta, ck_meta, k_real, v_real, ck_real, n_pad)
        h = _out_proj(attn, b_w_out[j].astype(BF16), h)
    return h
```

```python
import functools

import jax
import jax.numpy as jnp
from jax import lax
from jax.experimental import pallas as pl
from jax.experimental.pallas import tpu as pltpu

N_META = 16
HEAD_DIM = 64
NORM_EPS = 1e-6
GN_EPS = 64e-5
LANES = 128
CHUNK = 64
WKV_UNROLL = 10
WKV_SKEW = 1
FOX_Q_BLOCK = 1024
FOX_ROW_SLAB = 512
FRONT = 128
VMEM_LIMIT = 56 * 1024 * 1024

F32 = jnp.float32
BF16 = jnp.bfloat16
_NT = (((1,), (1,)), ((), ()))


def _dot(a, b):
    return jnp.dot(a, b, preferred_element_type=F32)


def _dot_nt(a, b):
    return lax.dot_general(a, b, _NT, preferred_element_type=F32)


def _split2(x):
    hi = x.astype(BF16)
    lo = (x - hi.astype(F32)).astype(BF16)
    return hi, lo


def _split3(x):
    hi = x.astype(BF16)
    r1 = x - hi.astype(F32)
    mid = r1.astype(BF16)
    lo = (r1 - mid.astype(F32)).astype(BF16)
    return hi, mid, lo


def _sigmoid(x):
    return 1.0 / (1.0 + jnp.exp(-x))


def _softplus(x):
    return jnp.maximum(x, 0.0) + jnp.log(1.0 + jnp.exp(-jnp.abs(x)))


def _pick_tile(n, cap, mult):
    best = mult
    for t in range(mult, cap + 1, mult):
        if n % t == 0:
            best = t
    return best


def _cparams(n_axes):
    return pltpu.CompilerParams(dimension_semantics=("arbitrary",) * n_axes,
                                vmem_limit_bytes=VMEM_LIMIT)


def _proj_shift_kernel(x_ref, g_ref, w_ref, mu_ref, o_ref, carry_ref):
    i = pl.program_id(2)
    x = x_ref[0]
    ms = jnp.mean(x * x, axis=-1, keepdims=True)
    xn = (x * lax.rsqrt(ms + NORM_EPS) * g_ref[...]).astype(BF16)
    p = _dot(xn, w_ref[...])
    tm = p.shape[0]

    @pl.when(i == 0)
    def _():
        carry_ref[...] = jnp.zeros_like(carry_ref)

    prev = pltpu.roll(p, 1, axis=0)
    row = lax.broadcasted_iota(jnp.int32, p.shape, 0)
    prev = jnp.where(row == 0, carry_ref[0:1, :], prev)
    o_ref[0] = p + mu_ref[...] * (prev - p)
    carry_ref[0:1, :] = p[tm - 1:tm, :]


def _proj_shift(h, g, w, mu, tn):
    B, T, D = h.shape
    NW = w.shape[1]
    tm = _pick_tile(T, 640, 64)
    return pl.pallas_call(
        _proj_shift_kernel,
        grid=(NW // tn, B, T // tm),
        in_specs=[
            pl.BlockSpec((1, tm, D), lambda j, b, i: (b, i, 0)),
            pl.BlockSpec((1, D), lambda j, b, i: (0, 0)),
            pl.BlockSpec((D, tn), lambda j, b, i: (0, j)),
            pl.BlockSpec((1, tn), lambda j, b, i: (0, j)),
        ],
        out_specs=pl.BlockSpec((1, tm, tn), lambda j, b, i: (b, i, j)),
        out_shape=jax.ShapeDtypeStruct((B, T, NW), F32),
        scratch_shapes=[pltpu.VMEM((8, tn), F32)],
        compiler_params=_cparams(3),
        name="proj_shift",
    )(h, g.reshape(1, D), w, mu.reshape(1, NW))


def _wkv_kernel(*refs, has_vres, n_chunks, unroll, skew):
    if has_vres:
        (r_ref, k_ref, v_ref, z_ref, wl_ref, al_ref, vl_ref, vf_ref,
         dup_ref, iup_ref, vup_ref, vec_ref, o_ref, h_ref) = refs
    else:
        (r_ref, k_ref, v_ref, z_ref, wl_ref, al_ref,
         dup_ref, iup_ref, vec_ref, o_ref, h_ref) = refs
        vl_ref = vf_ref = vup_ref = None
    C = CHUNK
    L = LANES

    @pl.when(pl.program_id(2) == 0)
    def _():
        h_ref[...] = jnp.zeros_like(h_ref)

    ri = lax.broadcasted_iota(jnp.int32, (L, L), 0)
    ci = lax.broadcasted_iota(jnp.int32, (L, L), 1)
    same_head = (ri < C) == (ci < C)
    rt_ = ri & (C - 1)
    ct_ = ci & (C - 1)
    hm = same_head.astype(F32)
    tri_s = same_head & (ct_ < rt_)
    tri_i = same_head & (ct_ <= rt_)
    eye = ri == ci
    eye_f = eye.astype(F32)
    r64 = lax.broadcasted_iota(jnp.int32, (C, C), 0)
    c64 = lax.broadcasted_iota(jnp.int32, (C, C), 1)
    tril64 = (c64 <= r64).astype(BF16)

    vec = vec_ref[...]
    decay_bias = vec[0:1]
    iclr_bias = vec[1:2]
    vres_bias = vec[2:3]
    k_k = vec[3:4]
    k_a = vec[4:5]
    r_k = vec[5:6]
    gn_w = vec[6:7]
    gn_b = vec[7:8]
    dup = dup_ref[...]
    iup = iup_ref[...]
    vup = vup_ref[...] if has_vres else None

    lane_h0 = lax.broadcasted_iota(jnp.int32, (C, L), 1) < HEAD_DIM

    def segsum(x):
        s0 = jnp.sum(jnp.where(lane_h0, x, 0.0), axis=-1, keepdims=True)
        s1 = jnp.sum(jnp.where(lane_h0, 0.0, x), axis=-1, keepdims=True)
        return jnp.where(lane_h0, s0, s1)

    def stack(x):
        return jnp.concatenate([x, x], axis=0) * hm

    def dup2(x):
        return jnp.concatenate([x, x], axis=0)

    zeros_ll = jnp.zeros((L, L), BF16)

    def chunk_stages(c):
        rows = pl.ds(pl.multiple_of(c * C, C), C)
        r = r_ref[0, rows, :]
        k = k_ref[0, rows, :]
        v = v_ref[0, rows, :]
        wl = wl_ref[0, rows, :]
        al = al_ref[0, rows, :]

        dx = decay_bias + _dot(jnp.tanh(wl).astype(BF16), dup)
        logw = -jnp.exp(-_softplus(-dx) - 0.5)
        a = _sigmoid(iclr_bias + _dot(al.astype(BF16), iup))
        if has_vres:
            vl = vl_ref[0, rows, :]
            vf = vf_ref[0, rows, :]
            v = v + (vf - v) * _sigmoid(vres_bias + _dot(vl.astype(BF16), vup))
        kkf = k * k_k
        yield
        kk = kkf / jnp.maximum(jnp.sqrt(segsum(kkf * kkf)), 1e-12)
        k2 = k * (1.0 + (a - 1.0) * k_a)
        b = kk * a

        lw_hi, lw_lo = _split2(logw)
        cum = _dot(tril64, lw_hi) + _dot(tril64, lw_lo)
        yield
        mid = cum[C // 2 - 1:C // 2, :]
        last = cum[C - 1:C, :]
        g_in = jnp.exp(cum - mid)
        g_ex = jnp.exp(cum - logw - mid)
        g_inv = jnp.exp(mid - cum)
        g_end = jnp.exp(last - cum)
        gm = jnp.exp(mid)
        g_c = jnp.exp(last)

        at_m = stack(-kk * g_ex)
        rt_m = stack(r * g_in)
        v_mb = stack(v).astype(BF16)
        lhs = jnp.concatenate([at_m, rt_m], axis=0).astype(BF16)
        rhs = jnp.concatenate([dup2(b * g_inv), dup2(k2 * g_inv)], axis=0).astype(BF16)
        A = _dot_nt(lhs, rhs)
        yield
        a_ab = jnp.where(tri_s, A[0:L, 0:L], 0.0)
        a_ak = jnp.where(tri_s, A[0:L, L:2 * L], 0.0).astype(BF16)
        a_rb = jnp.where(tri_i, A[L:2 * L, 0:L], 0.0).astype(BF16)
        a_rk = jnp.where(tri_i, A[L:2 * L, L:2 * L], 0.0).astype(BF16)
        AV = _dot(a_ak, v_mb)

        Pb = a_ab.astype(BF16)
        Sm = eye_f + a_ab
        Pb = _dot(Pb, Pb).astype(BF16)
        yield
        for _ in range(4):
            PS = _dot(Pb, jnp.concatenate([Pb, Sm.astype(BF16)], axis=1))
            yield
            Pb = PS[:, 0:L].astype(BF16)
            Sm = Sm + PS[:, L:2 * L]
        Sm = Sm + _dot(Pb, Sm.astype(BF16))
        yield

        X = _dot(Sm.astype(BF16),
                 jnp.concatenate([(at_m * gm).astype(BF16), AV.astype(BF16)], axis=1))
        bk = jnp.concatenate([stack(b * g_end), stack(k2 * g_end)], axis=0)
        lhs_big = jnp.concatenate(
            [jnp.concatenate([a_rb, a_rk], axis=1), bk.T.astype(BF16)], axis=0)
        yield
        rhs_big = jnp.concatenate(
            [X.astype(BF16), jnp.concatenate([zeros_ll, v_mb], axis=1)], axis=0)
        big = _dot(lhs_big, rhs_big)
        yield
        rm = big[:, 0:L] + jnp.concatenate([rt_m * gm, eye_f * g_c], axis=0)
        bonus = segsum(r * k2 * r_k) * v
        yield (rows, rm.astype(BF16), big, bonus)

    def epilogue_stages(rows, y, bonus):
        mean = segsum(y) * (1.0 / HEAD_DIM)
        yield
        d = y - mean
        var = segsum(d * d) * (1.0 / HEAD_DIM)
        yield
        yn = d * lax.rsqrt(var + GN_EPS) * gn_w + gn_b
        z = z_ref[0, rows, :]
        out = (yn + bonus) * (z * _sigmoid(z))
        o_ref[0, rows, :] = out.astype(o_ref.dtype)
        yield True

    def full_chunk(c, state):
        gen = chunk_stages(c)
        out = next(gen)
        while out is None:
            yield
            out = next(gen)
        rows, rm_b, big, bonus = out
        Hb = state["H"].astype(BF16)
        Hn = _dot(rm_b[L:2 * L], Hb) + big[L:2 * L, L:2 * L]
        state["H"] = Hn
        yield
        Ym = _dot(rm_b[0:L], Hb) + big[0:L, L:2 * L]
        yield
        yield from epilogue_stages(rows, Ym[0:C] + Ym[C:2 * C], bonus)

    def body(i, carry):
        state = {"H": h_ref[...]}
        gens = [full_chunk(i * unroll + u, state) for u in range(unroll)]
        done = [False] * unroll
        t = 0
        while not all(done):
            for u, g in enumerate(gens):
                if not done[u] and t >= skew * u:
                    done[u] = next(g) is True
            t += 1
        h_ref[...] = state["H"]
        return carry

    lax.fori_loop(0, n_chunks // unroll, body, 0)


def _wkv(proj_main, proj_lora, v_first, dup, iup, vup, vec):
    B, T, D4 = proj_main.shape
    D = D4 // 4
    npair = D // LANES
    tb = _pick_tile(T, 640, CHUNK)
    has_vres = v_first is not None
    row_blk = lambda off: pl.BlockSpec((1, tb, LANES), lambda b, p, t, off=off: (b, t, off + p))
    lora_blk = lambda j: pl.BlockSpec((1, tb, LANES), lambda b, p, t, j=j: (b, t, j))
    up_blk = pl.BlockSpec((LANES, LANES), lambda b, p, t: (0, p))
    in_specs = [row_blk(0), row_blk(npair), row_blk(2 * npair), row_blk(3 * npair),
                lora_blk(0), lora_blk(1)]
    args = [proj_main, proj_main, proj_main, proj_main, proj_lora, proj_lora]
    if has_vres:
        in_specs += [lora_blk(2), row_blk(2 * npair)]
        args += [proj_lora, v_first]
    in_specs += [up_blk, up_blk]
    args += [dup, iup]
    if has_vres:
        in_specs += [up_blk]
        args += [vup]
    in_specs += [pl.BlockSpec((8, LANES), lambda b, p, t: (0, p))]
    args += [vec]
    return pl.pallas_call(
        functools.partial(_wkv_kernel, has_vres=has_vres, n_chunks=tb // CHUNK,
                          unroll=_pick_tile(tb // CHUNK, WKV_UNROLL, 1), skew=WKV_SKEW),
        grid=(B, npair, T // tb),
        in_specs=in_specs,
        out_specs=pl.BlockSpec((1, tb, LANES), lambda b, p, t: (b, t, p)),
        out_shape=jax.ShapeDtypeStruct((B, T, D), BF16),
        scratch_shapes=[pltpu.VMEM((LANES, LANES), F32)],
        compiler_params=_cparams(3),
        name="wkv7",
    )(*args)


def _out_proj_kernel(a_ref, w_ref, h_ref, o_ref):
    o_ref[0] = h_ref[0] + _dot(a_ref[0], w_ref[...])


def _out_proj(act, w, h):
    B, T, D = h.shape
    tm = _pick_tile(T, 640, 64)
    return pl.pallas_call(
        _out_proj_kernel,
        grid=(B, T // tm),
        in_specs=[
            pl.BlockSpec((1, tm, D), lambda b, i: (b, i, 0)),
            pl.BlockSpec((D, D), lambda b, i: (0, 0)),
            pl.BlockSpec((1, tm, D), lambda b, i: (b, i, 0)),
        ],
        out_specs=pl.BlockSpec((1, tm, D), lambda b, i: (b, i, 0)),
        out_shape=jax.ShapeDtypeStruct((B, T, D), F32),
        compiler_params=_cparams(2),
        name="out_proj",
    )(act, w, h)


def _pair_rms(xg, gain2, ones_bd, scale):
    hi, lo = _split2(xg * xg)
    ms = (_dot(hi, ones_bd) + _dot(lo, ones_bd)) * (1.0 / HEAD_DIM)
    return xg * lax.rsqrt(ms + NORM_EPS) * (gain2 * scale)


def _ones_bd():
    ri = lax.broadcasted_iota(jnp.int32, (LANES, LANES), 0)
    ci = lax.broadcasted_iota(jnp.int32, (LANES, LANES), 1)
    return ((ri < HEAD_DIM) == (ci < HEAD_DIM)).astype(BF16)


BIAS_LANE = HEAD_DIM
N_SPLIT = 3
LOG2E = 1.4426950408889634


def _bias_select(n_heads, first_lane, sign):
    m = jnp.zeros((N_SPLIT * LANES, n_heads * LANES), F32)
    for j in range(N_SPLIT):
        rows = j * LANES + jnp.arange(n_heads)
        cols = jnp.arange(n_heads) * LANES + first_lane + j
        m = m.at[rows, cols].set(sign)
    return m.astype(BF16)


def _bias_lanes(c, select):
    hi, mid, lo = _split3(c * LOG2E)
    return _dot(jnp.concatenate([hi, mid, lo], axis=1), select)


def _lane_range(lo, hi):
    lane = lax.broadcasted_iota(jnp.int32, (1, LANES), 1)
    return ((lane >= lo) & (lane < hi)).astype(F32)


def _kv_kernel(x_ref, g_ref, w_ref, fb_ref, kn_ref, tril_ref, sel_ref, k_ref, v_ref, c_ref,
               carry_ref, *, n_pad):
    i = pl.program_id(1)
    x = x_ref[0]
    tm, D = x.shape
    ms = jnp.mean(x * x, axis=-1, keepdims=True)
    xn = (x * lax.rsqrt(ms + NORM_EPS) * g_ref[...]).astype(BF16)
    p = _dot(xn, w_ref[...])

    @pl.when(i == 0)
    def _():
        carry_ref[...] = jnp.zeros_like(carry_ref)

    row = lax.broadcasted_iota(jnp.int32, (tm, LANES), 0) + i * tm
    logf = -_softplus(-(p[:, 2 * D:2 * D + LANES] + fb_ref[...]))
    logf = jnp.where(row >= n_pad, logf, 0.0)
    hi, mid, lo = _split3(logf)
    tril = tril_ref[...]
    c = _dot(tril, hi) + _dot(tril, mid) + _dot(tril, lo) + carry_ref[0:1, :]
    c_ref[0] = c
    carry_ref[0:1, :] = c[tm - 1:tm, :]

    bias = _bias_lanes(c, sel_ref[...])
    head_lanes = lax.broadcasted_iota(jnp.int32, (1, LANES), 1) < HEAD_DIM
    k_const = _lane_range(BIAS_LANE, BIAS_LANE + N_SPLIT)
    v_const = _lane_range(BIAS_LANE, BIAS_LANE + 1)
    ones_bd = _ones_bd()
    for g in range(D // LANES):
        kn = _pair_rms(p[:, g * LANES:(g + 1) * LANES], kn_ref[...], ones_bd, 1.0)
        vg = p[:, D + g * LANES:D + (g + 1) * LANES]
        for hh in range(2):
            cols = slice((2 * g + hh) * LANES, (2 * g + hh + 1) * LANES)
            kh = kn if hh == 0 else pltpu.roll(kn, HEAD_DIM, axis=1)
            vh = vg if hh == 0 else pltpu.roll(vg, HEAD_DIM, axis=1)
            k_ref[0, :, cols] = jnp.where(head_lanes, kh, bias[:, cols] + k_const).astype(BF16)
            v_ref[0, :, cols] = jnp.where(head_lanes, vh, v_const).astype(BF16)


def _shared_kv(h, g, w, fbias, knorm2, n_pad):
    B, T, D = h.shape
    NW = w.shape[1]
    H = D // HEAD_DIM
    tm = _pick_tile(T, 640, 64)
    tril = jnp.tril(jnp.ones((tm, tm), BF16))
    sel = _bias_select(H, BIAS_LANE + N_SPLIT, -1.0)
    return pl.pallas_call(
        functools.partial(_kv_kernel, n_pad=n_pad),
        grid=(B, T // tm),
        in_specs=[
            pl.BlockSpec((1, tm, D), lambda b, i: (b, i, 0)),
            pl.BlockSpec((1, D), lambda b, i: (0, 0)),
            pl.BlockSpec((D, NW), lambda b, i: (0, 0)),
            pl.BlockSpec((1, LANES), lambda b, i: (0, 0)),
            pl.BlockSpec((1, LANES), lambda b, i: (0, 0)),
            pl.BlockSpec((tm, tm), lambda b, i: (0, 0)),
            pl.BlockSpec((N_SPLIT * LANES, H * LANES), lambda b, i: (0, 0)),
        ],
        out_specs=[
            pl.BlockSpec((1, tm, H * LANES), lambda b, i: (b, i, 0)),
            pl.BlockSpec((1, tm, H * LANES), lambda b, i: (b, i, 0)),
            pl.BlockSpec((1, tm, LANES), lambda b, i: (b, i, 0)),
        ],
        out_shape=[
            jax.ShapeDtypeStruct((B, T, H * LANES), BF16),
            jax.ShapeDtypeStruct((B, T, H * LANES), BF16),
            jax.ShapeDtypeStruct((B, T, LANES), F32),
        ],
        scratch_shapes=[pltpu.VMEM((8, LANES), F32)],
        compiler_params=_cparams(2),
        name="shared_kv",
    )(h, g.reshape(1, D), w, fbias, knorm2, tril, sel)


def _qz_kernel(x_ref, g_ref, w_ref, qn_ref, c_ref, sel_ref, q_ref, z_ref):
    x = x_ref[0]
    D = x.shape[1]
    ms = jnp.mean(x * x, axis=-1, keepdims=True)
    xn = (x * lax.rsqrt(ms + NORM_EPS) * g_ref[...]).astype(BF16)
    p = _dot(xn, w_ref[...])
    z_ref[0] = p[:, D:2 * D]

    bias = _bias_lanes(c_ref[0], sel_ref[...])
    head_lanes = lax.broadcasted_iota(jnp.int32, (1, LANES), 1) < HEAD_DIM
    q_const = _lane_range(BIAS_LANE + N_SPLIT, BIAS_LANE + 2 * N_SPLIT)
    ones_bd = _ones_bd()
    for g in range(D // LANES):
        qn = _pair_rms(p[:, g * LANES:(g + 1) * LANES], qn_ref[...], ones_bd,
                       HEAD_DIM ** -0.5 * LOG2E)
        for hh in range(2):
            cols = slice((2 * g + hh) * LANES, (2 * g + hh + 1) * LANES)
            qh = qn if hh == 0 else pltpu.roll(qn, HEAD_DIM, axis=1)
            q_ref[0, :, cols] = jnp.where(head_lanes, qh, bias[:, cols] + q_const).astype(BF16)


def _qz_proj(h, g, w, qnorm2, c_q):
    B, S, D = h.shape
    H = D // HEAD_DIM
    tm = _pick_tile(S, 512, 64)
    sel = _bias_select(H, BIAS_LANE, 1.0)
    return pl.pallas_call(
        _qz_kernel,
        grid=(B, S // tm),
        in_specs=[
            pl.BlockSpec((1, tm, D), lambda b, i: (b, i, 0)),
            pl.BlockSpec((1, D), lambda b, i: (0, 0)),
            pl.BlockSpec((D, 2 * D), lambda b, i: (0, 0)),
            pl.BlockSpec((1, LANES), lambda b, i: (0, 0)),
            pl.BlockSpec((1, tm, LANES), lambda b, i: (b, i, 0)),
            pl.BlockSpec((N_SPLIT * LANES, H * LANES), lambda b, i: (0, 0)),
        ],
        out_specs=[
            pl.BlockSpec((1, tm, H * LANES), lambda b, i: (b, i, 0)),
            pl.BlockSpec((1, tm, D), lambda b, i: (b, i, 0)),
        ],
        out_shape=[
            jax.ShapeDtypeStruct((B, S, H * LANES), BF16),
            jax.ShapeDtypeStruct((B, S, D), F32),
        ],
        compiler_params=_cparams(2),
        name="qz_proj",
    )(h, g.reshape(1, D), w, qnorm2, c_q, sel)


def _fox_kernel(q_ref, z_ref, k_ref, v_ref, o_ref, *, tq, rs, n_pad):
    qi = pl.program_id(2)
    n_slab = tq // rs
    lane = lax.broadcasted_iota(jnp.int32, (1, LANES), 1)
    col_m = lax.broadcasted_iota(jnp.int32, (1, FRONT), 1)
    causal = (lax.broadcasted_iota(jnp.int32, (rs, rs), 1)
              <= lax.broadcasted_iota(jnp.int32, (rs, rs), 0))
    neg = -jnp.inf

    def row_max(s):
        t = s[:, 0:LANES]
        for j in range(1, s.shape[1] // LANES):
            t = jnp.maximum(t, s[:, j * LANES:(j + 1) * LANES])
        return jnp.max(t, axis=-1, keepdims=True)

    def online(carry, s, vs):
        m, acc = carry
        m_new = jnp.maximum(m, row_max(s))
        p = jnp.exp2(s - m_new).astype(BF16)
        acc = acc * jnp.exp2(m - m_new) + _dot(p, vs)
        return m_new, acc

    chains = [(slice(hh * LANES, (hh + 1) * LANES), r) for hh in range(2) for r in range(n_slab)]
    qs = [q_ref[0, r * rs:(r + 1) * rs, cols] for cols, r in chains]

    def visit(carry, rows, active, masked):
        ss = {c: _dot_nt(qs[c], k_ref[0, rows, chains[c][0]]) for c in active}
        carry = list(carry)
        for c in active:
            s = jnp.where(causal, ss[c], neg) if c in masked else ss[c]
            carry[c] = online(carry[c], s, v_ref[0, rows, chains[c][0]])
        return tuple(carry)

    carry = []
    for c, (cols, _) in enumerate(chains):
        s = jnp.where(col_m >= n_pad, _dot_nt(qs[c], k_ref[0, 0:FRONT, cols]), neg)
        m = jnp.max(s, axis=-1, keepdims=True)
        carry.append((m, _dot(jnp.exp2(s - m).astype(BF16), v_ref[0, 0:FRONT, cols])))

    def key_rows(kb):
        return pl.ds(pl.multiple_of(FRONT + kb * rs, LANES), rs)

    everyone = range(len(chains))

    def body(kb, carry):
        for j in range(n_slab):
            carry = visit(carry, key_rows(kb * n_slab + j), everyone, ())
        return carry

    carry = lax.fori_loop(0, qi, body, tuple(carry))
    for j in range(n_slab):
        active = [c for c, (_, r) in enumerate(chains) if r >= j]
        masked = [c for c, (_, r) in enumerate(chains) if r == j]
        carry = visit(carry, key_rows(qi * n_slab + j), active, masked)

    outs = []
    for hh in range(2):
        acc = jnp.concatenate([carry[c][1] for c, (cols, _) in enumerate(chains)
                               if cols.start == hh * LANES], axis=0)
        outs.append(acc / acc[:, BIAS_LANE:BIAS_LANE + 1])

    z = z_ref[0]
    out = jnp.where(lane < HEAD_DIM, outs[0], pltpu.roll(outs[1], HEAD_DIM, axis=1))
    o_ref[0] = (out * (z * _sigmoid(z))).astype(o_ref.dtype)


def _fox_attention(q, z, k, v, n_pad):
    B, S, D = z.shape
    T = k.shape[1]
    npair = D // LANES
    tq = _pick_tile(S, FOX_Q_BLOCK, 128)
    rs = _pick_tile(tq, FOX_ROW_SLAB, 128)
    return pl.pallas_call(
        functools.partial(_fox_kernel, tq=tq, rs=rs, n_pad=n_pad),
        grid=(B, npair, S // tq),
        in_specs=[
            pl.BlockSpec((1, tq, 2 * LANES), lambda b, p, i: (b, i, p)),
            pl.BlockSpec((1, tq, LANES), lambda b, p, i: (b, i, p)),
            pl.BlockSpec((1, T, 2 * LANES), lambda b, p, i: (b, 0, p)),
            pl.BlockSpec((1, T, 2 * LANES), lambda b, p, i: (b, 0, p)),
        ],
        out_specs=pl.BlockSpec((1, tq, LANES), lambda b, p, i: (b, i, p)),
        out_shape=jax.ShapeDtypeStruct((B, S, D), BF16),
        compiler_params=_cparams(3),
        name="fox_attention",
    )(q, z, k, v)


def _pad_rows(m, rows):
    return jnp.pad(m, ((0, rows - m.shape[0]), (0, 0)))


def kernel(x, meta_tokens, a_norm, a_w_in, a_shift_mu, a_vres_down, a_vres_mu, a_vres_up, a_vres_bias, a_decay_up, a_decay_bias, a_iclr_up, a_iclr_bias, a_k_k, a_k_a, a_r_k, a_gn_w, a_gn_b, a_w_out, kv_norm, kv_w, kv_f_bias, k_norm, b_norm, b_w_in, b_q_norm, b_w_out):
    B, S, D = x.shape
    H = D // HEAD_DIM
    npair = D // LANES
    n_a = a_w_in.shape[0]
    n_b = b_w_in.shape[0]
    n_pad = FRONT - N_META
    lora_d = a_decay_up.shape[1]
    lora_i = a_iclr_up.shape[1]

    meta = jnp.broadcast_to(meta_tokens.astype(x.dtype)[None], (B, N_META, D))
    h = jnp.concatenate([jnp.zeros((B, n_pad, D), x.dtype), meta, x], axis=1)

    v_first = None
    for l in range(n_a):
        w_in = a_w_in[l]
        mu = a_shift_mu[l]
        w_main = w_in[:, :4 * D].astype(BF16)
        w_lora = jnp.zeros((D, 3 * LANES), F32)
        mu_lora = jnp.zeros((3 * LANES,), F32)
        w_lora = w_lora.at[:, 0:lora_d].set(w_in[:, 4 * D:4 * D + lora_d])
        mu_lora = mu_lora.at[0:lora_d].set(mu[4 * D:4 * D + lora_d])
        w_lora = w_lora.at[:, LANES:LANES + lora_i].set(w_in[:, 4 * D + lora_d:4 * D + lora_d + lora_i])
        mu_lora = mu_lora.at[LANES:LANES + lora_i].set(mu[4 * D + lora_d:4 * D + lora_d + lora_i])
        vup = None
        vres_bias = jnp.zeros((D,), F32)
        if l > 0:
            lora_v = a_vres_down.shape[2]
            w_lora = w_lora.at[:, 2 * LANES:2 * LANES + lora_v].set(a_vres_down[l - 1])
            mu_lora = mu_lora.at[2 * LANES:2 * LANES + lora_v].set(a_vres_mu[l - 1])
            vup = _pad_rows(a_vres_up[l - 1], LANES).astype(BF16)
            vres_bias = a_vres_bias[l - 1]
        proj_main = _proj_shift(h, a_norm[l], w_main, mu[:4 * D], 2 * D)
        proj_lora = _proj_shift(h, a_norm[l], w_lora.astype(BF16), mu_lora, 3 * LANES)
        vec = jnp.stack([a_decay_bias[l], a_iclr_bias[l], vres_bias, a_k_k[l], a_k_a[l],
                         a_r_k[l], a_gn_w[l], a_gn_b[l]], axis=0)
        gated = _wkv(proj_main, proj_lora, v_first if l > 0 else None,
                     _pad_rows(a_decay_up[l], LANES).astype(BF16),
                     _pad_rows(a_iclr_up[l], LANES).astype(BF16), vup, vec)
        if l == 0:
            v_first = proj_main
        h = _out_proj(gated, a_w_out[l].astype(BF16), h)

    w_kv = jnp.zeros((D, 2 * D + LANES), F32).at[:, :2 * D + H].set(kv_w).astype(BF16)
    fbias = jnp.zeros((1, LANES), F32).at[0, :H].set(kv_f_bias)
    knorm2 = jnp.concatenate([k_norm, k_norm]).reshape(1, LANES)
    k_sh, v_sh, c_all = _shared_kv(h, kv_norm, w_kv, fbias, knorm2, n_pad)

    c_q = c_all[:, FRONT:]
    h = h[:, FRONT:]
    for j in range(n_b):
        qnorm2 = jnp.concatenate([b_q_norm[j], b_q_norm[j]]).reshape(1, LANES)
        q, z = _qz_proj(h, b_norm[j], b_w_in[j].astype(BF16), qnorm2, c_q)
        attn = _fox_attention(q, z, k_sh, v_sh, n_pad)
        h = _out_proj(attn, b_w_out[j].astype(BF16), h)
    return h
```

```python
import functools

import jax
import jax.numpy as jnp
from jax import lax
from jax.experimental import pallas as pl
from jax.experimental.pallas import tpu as pltpu

N_META = 16
HEAD_DIM = 64
NORM_EPS = 1e-6
GN_EPS = 64e-5
LANES = 128
CHUNK = 64
WKV_UNROLL = 10
WKV_SKEW = 1
FOX_Q_BLOCK = 1024
FOX_ROW_SLAB = 512
FRONT = 128
VMEM_LIMIT = 56 * 1024 * 1024

F32 = jnp.float32
BF16 = jnp.bfloat16
_NT = (((1,), (1,)), ((), ()))


def _dot(a, b):
    return jnp.dot(a, b, preferred_element_type=F32)


def _dot_nt(a, b):
    return lax.dot_general(a, b, _NT, preferred_element_type=F32)


def _split2(x):
    hi = x.astype(BF16)
    lo = (x - hi.astype(F32)).astype(BF16)
    return hi, lo


def _split3(x):
    hi = x.astype(BF16)
    r1 = x - hi.astype(F32)
    mid = r1.astype(BF16)
    lo = (r1 - mid.astype(F32)).astype(BF16)
    return hi, mid, lo


def _sigmoid(x):
    return 1.0 / (1.0 + jnp.exp(-x))


def _softplus(x):
    return jnp.maximum(x, 0.0) + jnp.log(1.0 + jnp.exp(-jnp.abs(x)))


def _pick_tile(n, cap, mult):
    best = mult
    for t in range(mult, cap + 1, mult):
        if n % t == 0:
            best = t
    return best


def _cparams(n_axes):
    return pltpu.CompilerParams(dimension_semantics=("arbitrary",) * n_axes,
                                vmem_limit_bytes=VMEM_LIMIT)


def _proj_shift_kernel(x_ref, g_ref, w_ref, mu_ref, o_ref, carry_ref):
    i = pl.program_id(2)
    x = x_ref[0]
    ms = jnp.mean(x * x, axis=-1, keepdims=True)
    xn = (x * lax.rsqrt(ms + NORM_EPS) * g_ref[...]).astype(BF16)
    p = _dot(xn, w_ref[...])
    tm = p.shape[0]

    @pl.when(i == 0)
    def _():
        carry_ref[...] = jnp.zeros_like(carry_ref)

    prev = pltpu.roll(p, 1, axis=0)
    row = lax.broadcasted_iota(jnp.int32, p.shape, 0)
    prev = jnp.where(row == 0, carry_ref[0:1, :], prev)
    o_ref[0] = p + mu_ref[...] * (prev - p)
    carry_ref[0:1, :] = p[tm - 1:tm, :]


def _proj_shift(h, g, w, mu, tn):
    B, T, D = h.shape
    NW = w.shape[1]
    tm = _pick_tile(T, 640, 64)
    return pl.pallas_call(
        _proj_shift_kernel,
        grid=(NW // tn, B, T // tm),
        in_specs=[
            pl.BlockSpec((1, tm, D), lambda j, b, i: (b, i, 0)),
            pl.BlockSpec((1, D), lambda j, b, i: (0, 0)),
            pl.BlockSpec((D, tn), lambda j, b, i: (0, j)),
            pl.BlockSpec((1, tn), lambda j, b, i: (0, j)),
        ],
        out_specs=pl.BlockSpec((1, tm, tn), lambda j, b, i: (b, i, j)),
        out_shape=jax.ShapeDtypeStruct((B, T, NW), F32),
        scratch_shapes=[pltpu.VMEM((8, tn), F32)],
        compiler_params=_cparams(3),
        name="proj_shift",
    )(h, g.reshape(1, D), w, mu.reshape(1, NW))


def _wkv_kernel(*refs, has_vres, n_chunks, unroll, skew):
    if has_vres:
        (r_ref, k_ref, v_ref, z_ref, wl_ref, al_ref, vl_ref, vf_ref,
         dui_ref, vup_ref, vec_ref, o_ref, h_ref) = refs
    else:
        (r_ref, k_ref, v_ref, z_ref, wl_ref, al_ref,
         dui_ref, vec_ref, o_ref, h_ref) = refs
        vl_ref = vf_ref = vup_ref = None
    C = CHUNK
    L = LANES

    n_streams = h_ref.shape[0]

    @pl.when(pl.program_id(1) == 0)
    def _():
        h_ref[...] = jnp.zeros_like(h_ref)

    ri = lax.broadcasted_iota(jnp.int32, (L, L), 0)
    ci = lax.broadcasted_iota(jnp.int32, (L, L), 1)
    same_head = (ri < C) == (ci < C)
    rt_ = ri & (C - 1)
    ct_ = ci & (C - 1)
    hm = same_head.astype(F32)
    tri_s = same_head & (ct_ < rt_)
    tri_i = same_head & (ct_ <= rt_)
    eye = ri == ci
    eye_f = eye.astype(F32)
    r64 = lax.broadcasted_iota(jnp.int32, (C, C), 0)
    c64 = lax.broadcasted_iota(jnp.int32, (C, C), 1)
    tril64 = (c64 <= r64).astype(BF16)

    vec = vec_ref[...]
    decay_bias = vec[0:1]
    iclr_bias = vec[1:2]
    vres_bias = vec[2:3]
    k_k = vec[3:4]
    k_a = vec[4:5]
    r_k = vec[5:6]
    gn_w = vec[6:7]
    gn_b = vec[7:8]
    dui = dui_ref[...]
    vup = vup_ref[...] if has_vres else None

    lane_h0 = lax.broadcasted_iota(jnp.int32, (C, L), 1) < HEAD_DIM

    def segsum(x):
        s0 = jnp.sum(jnp.where(lane_h0, x, 0.0), axis=-1, keepdims=True)
        s1 = jnp.sum(jnp.where(lane_h0, 0.0, x), axis=-1, keepdims=True)
        return jnp.where(lane_h0, s0, s1)

    def stack(x):
        return jnp.concatenate([x, x], axis=0) * hm

    def dup2(x):
        return jnp.concatenate([x, x], axis=0)

    zeros_ll = jnp.zeros((L, L), BF16)

    def chunk_stages(bb, c):
        rows = pl.ds(pl.multiple_of(c * C, C), C)
        r = r_ref[bb, rows, :]
        k = k_ref[bb, rows, :]
        v = v_ref[bb, rows, :]
        wl = wl_ref[bb, rows, :]
        al = al_ref[bb, rows, :]

        up = _dot(jnp.concatenate([jnp.tanh(wl), al], axis=1).astype(BF16), dui)
        logw = -jnp.exp(-_softplus(-(decay_bias + up[:, 0:L])) - 0.5)
        a = _sigmoid(iclr_bias + up[:, L:2 * L])
        if has_vres:
            vl = vl_ref[bb, rows, :]
            vf = vf_ref[bb, rows, :]
            v = v + (vf - v) * _sigmoid(vres_bias + _dot(vl.astype(BF16), vup))
        kkf = k * k_k
        yield
        kk = kkf / jnp.maximum(jnp.sqrt(segsum(kkf * kkf)), 1e-12)
        k2 = k * (1.0 + (a - 1.0) * k_a)
        b = kk * a

        cum2 = _dot(tril64, jnp.concatenate(_split2(logw), axis=1))
        cum = cum2[:, 0:L] + cum2[:, L:2 * L]
        yield
        mid = cum[C // 2 - 1:C // 2, :]
        last = cum[C - 1:C, :]
        g_in = jnp.exp(cum - mid)
        g_ex = jnp.exp(cum - logw - mid)
        g_inv = jnp.exp(mid - cum)
        g_end = jnp.exp(last - cum)
        gm = jnp.exp(mid)
        g_c = jnp.exp(last)

        at_m = stack(-kk * g_ex)
        rt_m = stack(r * g_in)
        v_mb = stack(v).astype(BF16)
        lhs = jnp.concatenate([at_m, rt_m], axis=0).astype(BF16)
        rhs = jnp.concatenate([dup2(b * g_inv), dup2(k2 * g_inv)], axis=0).astype(BF16)
        A = _dot_nt(lhs, rhs)
        yield
        a_ab = jnp.where(tri_s, A[0:L, 0:L], 0.0)
        a_ak = jnp.where(tri_s, A[0:L, L:2 * L], 0.0).astype(BF16)
        a_rb = jnp.where(tri_i, A[L:2 * L, 0:L], 0.0).astype(BF16)
        a_rk = jnp.where(tri_i, A[L:2 * L, L:2 * L], 0.0).astype(BF16)
        AV = _dot(a_ak, v_mb)

        Pb = a_ab.astype(BF16)
        Sm = eye_f + a_ab
        Pb = _dot(Pb, Pb).astype(BF16)
        yield
        for _ in range(4):
            PS = _dot(Pb, jnp.concatenate([Pb, Sm.astype(BF16)], axis=1))
            yield
            Pb = PS[:, 0:L].astype(BF16)
            Sm = Sm + PS[:, L:2 * L]
        Sm = Sm + _dot(Pb, Sm.astype(BF16))
        yield

        X = _dot(Sm.astype(BF16),
                 jnp.concatenate([(at_m * gm).astype(BF16), AV.astype(BF16)], axis=1))
        bk = jnp.concatenate([stack(b * g_end), stack(k2 * g_end)], axis=0)
        lhs_big = jnp.concatenate(
            [jnp.concatenate([a_rb, a_rk], axis=1), bk.T.astype(BF16)], axis=0)
        yield
        rhs_big = jnp.concatenate(
            [X.astype(BF16), jnp.concatenate([zeros_ll, v_mb], axis=1)], axis=0)
        big = _dot(lhs_big, rhs_big)
        yield
        rm = big[:, 0:L] + jnp.concatenate([rt_m * gm, eye_f * g_c], axis=0)
        bonus = segsum(r * k2 * r_k) * v
        yield (rows, rm.astype(BF16), big, bonus)

    def epilogue_stages(bb, rows, y, bonus):
        mean = segsum(y) * (1.0 / HEAD_DIM)
        yield
        d = y - mean
        var = segsum(d * d) * (1.0 / HEAD_DIM)
        yield
        yn = d * lax.rsqrt(var + GN_EPS) * gn_w + gn_b
        z = z_ref[bb, rows, :]
        out = (yn + bonus) * (z * _sigmoid(z))
        o_ref[bb, rows, :] = out.astype(o_ref.dtype)
        yield True

    def full_chunk(bb, c, state):
        gen = chunk_stages(bb, c)
        out = next(gen)
        while out is None:
            yield
            out = next(gen)
        rows, rm_b, big, bonus = out
        Hb = state["H"].astype(BF16)
        Hn = _dot(rm_b[L:2 * L], Hb) + big[L:2 * L, L:2 * L]
        state["H"] = Hn
        yield
        Ym = _dot(rm_b[0:L], Hb) + big[0:L, L:2 * L]
        yield
        yield from epilogue_stages(bb, rows, Ym[0:C] + Ym[C:2 * C], bonus)

    def body(i, carry):
        states = [{"H": h_ref[bb]} for bb in range(n_streams)]
        gens = [full_chunk(bb, i * unroll + u, states[bb])
                for u in range(unroll) for bb in range(n_streams)]
        done = [False] * len(gens)
        t = 0
        while not all(done):
            for g, gen in enumerate(gens):
                if not done[g] and t >= skew * g:
                    done[g] = next(gen) is True
            t += 1
        for bb in range(n_streams):
            h_ref[bb] = states[bb]["H"]
        return carry

    lax.fori_loop(0, n_chunks // unroll, body, 0)


def _pair_blockdiag(top, bot):
    npair = top.shape[1] // LANES
    t = top.reshape(LANES, npair, LANES)
    b = bot.reshape(LANES, npair, LANES)
    z = jnp.zeros_like(t)
    rows = [jnp.concatenate([t, z], axis=2), jnp.concatenate([z, b], axis=2)]
    return jnp.concatenate(rows, axis=0).reshape(2 * LANES, 2 * npair * LANES)


def _wkv(proj_main, proj_lora, v_first, dui, vup, vec):
    B, T, D4 = proj_main.shape
    D = D4 // 4
    npair = D // LANES
    tb = _pick_tile(T, 640, CHUNK)
    has_vres = v_first is not None
    row_blk = lambda off: pl.BlockSpec((B, tb, LANES), lambda p, t, off=off: (0, t, off + p))
    lora_blk = lambda j: pl.BlockSpec((B, tb, LANES), lambda p, t, j=j: (0, t, j))
    up_blk = pl.BlockSpec((LANES, LANES), lambda p, t: (0, p))
    in_specs = [row_blk(0), row_blk(npair), row_blk(2 * npair), row_blk(3 * npair),
                lora_blk(0), lora_blk(1)]
    args = [proj_main, proj_main, proj_main, proj_main, proj_lora, proj_lora]
    if has_vres:
        in_specs += [lora_blk(2), row_blk(2 * npair)]
        args += [proj_lora, v_first]
    in_specs += [pl.BlockSpec((2 * LANES, 2 * LANES), lambda p, t: (0, p))]
    args += [dui]
    if has_vres:
        in_specs += [up_blk]
        args += [vup]
    in_specs += [pl.BlockSpec((8, LANES), lambda p, t: (0, p))]
    args += [vec]
    return pl.pallas_call(
        functools.partial(_wkv_kernel, has_vres=has_vres, n_chunks=tb // CHUNK,
                          unroll=_pick_tile(tb // CHUNK, WKV_UNROLL, 1), skew=WKV_SKEW),
        grid=(npair, T // tb),
        in_specs=in_specs,
        out_specs=pl.BlockSpec((B, tb, LANES), lambda p, t: (0, t, p)),
        out_shape=jax.ShapeDtypeStruct((B, T, D), BF16),
        scratch_shapes=[pltpu.VMEM((B, LANES, LANES), F32)],
        compiler_params=_cparams(2),
        name="wkv7",
    )(*args)


def _out_proj_kernel(a_ref, w_ref, h_ref, o_ref):
    o_ref[0] = h_ref[0] + _dot(a_ref[0], w_ref[...])


def _out_proj(act, w, h):
    B, T, D = h.shape
    tm = _pick_tile(T, 640, 64)
    return pl.pallas_call(
        _out_proj_kernel,
        grid=(B, T // tm),
        in_specs=[
            pl.BlockSpec((1, tm, D), lambda b, i: (b, i, 0)),
            pl.BlockSpec((D, D), lambda b, i: (0, 0)),
            pl.BlockSpec((1, tm, D), lambda b, i: (b, i, 0)),
        ],
        out_specs=pl.BlockSpec((1, tm, D), lambda b, i: (b, i, 0)),
        out_shape=jax.ShapeDtypeStruct((B, T, D), F32),
        compiler_params=_cparams(2),
        name="out_proj",
    )(act, w, h)


def _pair_rms(xg, gain2, ones_bd, scale):
    hi, lo = _split2(xg * xg)
    ms = (_dot(hi, ones_bd) + _dot(lo, ones_bd)) * (1.0 / HEAD_DIM)
    return xg * lax.rsqrt(ms + NORM_EPS) * (gain2 * scale)


def _ones_bd():
    ri = lax.broadcasted_iota(jnp.int32, (LANES, LANES), 0)
    ci = lax.broadcasted_iota(jnp.int32, (LANES, LANES), 1)
    return ((ri < HEAD_DIM) == (ci < HEAD_DIM)).astype(BF16)


BIAS_LANE = HEAD_DIM
N_SPLIT = 3
LOG2E = 1.4426950408889634


def _bias_select(n_heads, first_lane, sign):
    m = jnp.zeros((N_SPLIT * LANES, n_heads * LANES), F32)
    for j in range(N_SPLIT):
        rows = j * LANES + jnp.arange(n_heads)
        cols = jnp.arange(n_heads) * LANES + first_lane + j
        m = m.at[rows, cols].set(sign)
    return m.astype(BF16)


def _bias_lanes(c, select):
    hi, mid, lo = _split3(c * LOG2E)
    return _dot(jnp.concatenate([hi, mid, lo], axis=1), select)


def _lane_range(lo, hi):
    lane = lax.broadcasted_iota(jnp.int32, (1, LANES), 1)
    return ((lane >= lo) & (lane < hi)).astype(F32)


def _kv_kernel(x_ref, g_ref, w_ref, fb_ref, kn_ref, tril_ref, sel_ref, k_ref, v_ref, c_ref,
               carry_ref, *, n_pad):
    i = pl.program_id(1)
    x = x_ref[0]
    tm, D = x.shape
    ms = jnp.mean(x * x, axis=-1, keepdims=True)
    xn = (x * lax.rsqrt(ms + NORM_EPS) * g_ref[...]).astype(BF16)
    p = _dot(xn, w_ref[...])

    @pl.when(i == 0)
    def _():
        carry_ref[...] = jnp.zeros_like(carry_ref)

    row = lax.broadcasted_iota(jnp.int32, (tm, LANES), 0) + i * tm
    logf = -_softplus(-(p[:, 2 * D:2 * D + LANES] + fb_ref[...]))
    logf = jnp.where(row >= n_pad, logf, 0.0)
    hi, mid, lo = _split3(logf)
    tril = tril_ref[...]
    c = _dot(tril, hi) + _dot(tril, mid) + _dot(tril, lo) + carry_ref[0:1, :]
    c_ref[0] = c
    carry_ref[0:1, :] = c[tm - 1:tm, :]

    bias = _bias_lanes(c, sel_ref[...])
    head_lanes = lax.broadcasted_iota(jnp.int32, (1, LANES), 1) < HEAD_DIM
    k_const = _lane_range(BIAS_LANE, BIAS_LANE + N_SPLIT)
    v_const = _lane_range(BIAS_LANE, BIAS_LANE + 1)
    ones_bd = _ones_bd()
    for g in range(D // LANES):
        kn = _pair_rms(p[:, g * LANES:(g + 1) * LANES], kn_ref[...], ones_bd, 1.0)
        vg = p[:, D + g * LANES:D + (g + 1) * LANES]
        for hh in range(2):
            cols = slice((2 * g + hh) * LANES, (2 * g + hh + 1) * LANES)
            kh = kn if hh == 0 else pltpu.roll(kn, HEAD_DIM, axis=1)
            vh = vg if hh == 0 else pltpu.roll(vg, HEAD_DIM, axis=1)
            k_ref[0, :, cols] = jnp.where(head_lanes, kh, bias[:, cols] + k_const).astype(BF16)
            v_ref[0, :, cols] = jnp.where(head_lanes, vh, v_const).astype(BF16)


def _shared_kv(h, g, w, fbias, knorm2, n_pad):
    B, T, D = h.shape
    NW = w.shape[1]
    H = D // HEAD_DIM
    tm = _pick_tile(T, 640, 64)
    tril = jnp.tril(jnp.ones((tm, tm), BF16))
    sel = _bias_select(H, BIAS_LANE + N_SPLIT, -1.0)
    return pl.pallas_call(
        functools.partial(_kv_kernel, n_pad=n_pad),
        grid=(B, T // tm),
        in_specs=[
            pl.BlockSpec((1, tm, D), lambda b, i: (b, i, 0)),
            pl.BlockSpec((1, D), lambda b, i: (0, 0)),
            pl.BlockSpec((D, NW), lambda b, i: (0, 0)),
            pl.BlockSpec((1, LANES), lambda b, i: (0, 0)),
            pl.BlockSpec((1, LANES), lambda b, i: (0, 0)),
            pl.BlockSpec((tm, tm), lambda b, i: (0, 0)),
            pl.BlockSpec((N_SPLIT * LANES, H * LANES), lambda b, i: (0, 0)),
        ],
        out_specs=[
            pl.BlockSpec((1, tm, H * LANES), lambda b, i: (b, i, 0)),
            pl.BlockSpec((1, tm, H * LANES), lambda b, i: (b, i, 0)),
            pl.BlockSpec((1, tm, LANES), lambda b, i: (b, i, 0)),
        ],
        out_shape=[
            jax.ShapeDtypeStruct((B, T, H * LANES), BF16),
            jax.ShapeDtypeStruct((B, T, H * LANES), BF16),
            jax.ShapeDtypeStruct((B, T, LANES), F32),
        ],
        scratch_shapes=[pltpu.VMEM((8, LANES), F32)],
        compiler_params=_cparams(2),
        name="shared_kv",
    )(h, g.reshape(1, D), w, fbias, knorm2, tril, sel)


def _qz_kernel(x_ref, g_ref, w_ref, qn_ref, c_ref, sel_ref, q_ref, z_ref):
    x = x_ref[0]
    D = x.shape[1]
    ms = jnp.mean(x * x, axis=-1, keepdims=True)
    xn = (x * lax.rsqrt(ms + NORM_EPS) * g_ref[...]).astype(BF16)
    p = _dot(xn, w_ref[...])
    z_ref[0] = p[:, D:2 * D]

    bias = _bias_lanes(c_ref[0], sel_ref[...])
    head_lanes = lax.broadcasted_iota(jnp.int32, (1, LANES), 1) < HEAD_DIM
    q_const = _lane_range(BIAS_LANE + N_SPLIT, BIAS_LANE + 2 * N_SPLIT)
    ones_bd = _ones_bd()
    for g in range(D // LANES):
        qn = _pair_rms(p[:, g * LANES:(g + 1) * LANES], qn_ref[...], ones_bd,
                       HEAD_DIM ** -0.5 * LOG2E)
        for hh in range(2):
            cols = slice((2 * g + hh) * LANES, (2 * g + hh + 1) * LANES)
            qh = qn if hh == 0 else pltpu.roll(qn, HEAD_DIM, axis=1)
            q_ref[0, :, cols] = jnp.where(head_lanes, qh, bias[:, cols] + q_const).astype(BF16)


def _qz_proj(h, g, w, qnorm2, c_q):
    B, S, D = h.shape
    H = D // HEAD_DIM
    tm = _pick_tile(S, 512, 64)
    sel = _bias_select(H, BIAS_LANE, 1.0)
    return pl.pallas_call(
        _qz_kernel,
        grid=(B, S // tm),
        in_specs=[
            pl.BlockSpec((1, tm, D), lambda b, i: (b, i, 0)),
            pl.BlockSpec((1, D), lambda b, i: (0, 0)),
            pl.BlockSpec((D, 2 * D), lambda b, i: (0, 0)),
            pl.BlockSpec((1, LANES), lambda b, i: (0, 0)),
            pl.BlockSpec((1, tm, LANES), lambda b, i: (b, i, 0)),
            pl.BlockSpec((N_SPLIT * LANES, H * LANES), lambda b, i: (0, 0)),
        ],
        out_specs=[
            pl.BlockSpec((1, tm, H * LANES), lambda b, i: (b, i, 0)),
            pl.BlockSpec((1, tm, D), lambda b, i: (b, i, 0)),
        ],
        out_shape=[
            jax.ShapeDtypeStruct((B, S, H * LANES), BF16),
            jax.ShapeDtypeStruct((B, S, D), F32),
        ],
        compiler_params=_cparams(2),
        name="qz_proj",
    )(h, g.reshape(1, D), w, qnorm2, c_q, sel)


def _fox_kernel(q_ref, z_ref, k_ref, v_ref, o_ref, *, tq, rs, n_pad):
    qi = pl.program_id(2)
    n_slab = tq // rs
    lane = lax.broadcasted_iota(jnp.int32, (1, LANES), 1)
    col_m = lax.broadcasted_iota(jnp.int32, (1, FRONT), 1)
    causal = (lax.broadcasted_iota(jnp.int32, (rs, rs), 1)
              <= lax.broadcasted_iota(jnp.int32, (rs, rs), 0))
    neg = -jnp.inf

    def row_max(s):
        t = s[:, 0:LANES]
        for j in range(1, s.shape[1] // LANES):
            t = jnp.maximum(t, s[:, j * LANES:(j + 1) * LANES])
        return jnp.max(t, axis=-1, keepdims=True)

    def online(carry, s, vs):
        m, acc = carry
        m_new = jnp.maximum(m, row_max(s))
        p = jnp.exp2((s - m_new).astype(BF16))
        acc = acc * jnp.exp2(m - m_new) + _dot(p, vs)
        return m_new, acc

    chains = [(slice(hh * LANES, (hh + 1) * LANES), r) for hh in range(2) for r in range(n_slab)]
    qs = [q_ref[0, r * rs:(r + 1) * rs, cols] for cols, r in chains]

    def visit(carry, rows, active, masked):
        ss = {c: _dot_nt(qs[c], k_ref[0, rows, chains[c][0]]) for c in active}
        carry = list(carry)
        for c in active:
            s = jnp.where(causal, ss[c], neg) if c in masked else ss[c]
            carry[c] = online(carry[c], s, v_ref[0, rows, chains[c][0]])
        return tuple(carry)

    carry = []
    for c, (cols, _) in enumerate(chains):
        s = jnp.where(col_m >= n_pad, _dot_nt(qs[c], k_ref[0, 0:FRONT, cols]), neg)
        m = jnp.max(s, axis=-1, keepdims=True)
        carry.append((m, _dot(jnp.exp2(s - m).astype(BF16), v_ref[0, 0:FRONT, cols])))

    def key_rows(kb):
        return pl.ds(pl.multiple_of(FRONT + kb * rs, LANES), rs)

    everyone = range(len(chains))

    def body(kb, carry):
        for j in range(n_slab):
            carry = visit(carry, key_rows(kb * n_slab + j), everyone, ())
        return carry

    carry = lax.fori_loop(0, qi, body, tuple(carry))
    for j in range(n_slab):
        active = [c for c, (_, r) in enumerate(chains) if r >= j]
        masked = [c for c, (_, r) in enumerate(chains) if r == j]
        carry = visit(carry, key_rows(qi * n_slab + j), active, masked)

    outs = []
    for hh in range(2):
        acc = jnp.concatenate([carry[c][1] for c, (cols, _) in enumerate(chains)
                               if cols.start == hh * LANES], axis=0)
        outs.append(acc / acc[:, BIAS_LANE:BIAS_LANE + 1])

    z = z_ref[0]
    out = jnp.where(lane < HEAD_DIM, outs[0], pltpu.roll(outs[1], HEAD_DIM, axis=1))
    o_ref[0] = (out * (z * _sigmoid(z))).astype(o_ref.dtype)


def _fox_attention(q, z, k, v, n_pad):
    B, S, D = z.shape
    T = k.shape[1]
    npair = D // LANES
    tq = _pick_tile(S, FOX_Q_BLOCK, 128)
    rs = _pick_tile(tq, FOX_ROW_SLAB, 128)
    return pl.pallas_call(
        functools.partial(_fox_kernel, tq=tq, rs=rs, n_pad=n_pad),
        grid=(B, npair, S // tq),
        in_specs=[
            pl.BlockSpec((1, tq, 2 * LANES), lambda b, p, i: (b, i, p)),
            pl.BlockSpec((1, tq, LANES), lambda b, p, i: (b, i, p)),
            pl.BlockSpec((1, T, 2 * LANES), lambda b, p, i: (b, 0, p)),
            pl.BlockSpec((1, T, 2 * LANES), lambda b, p, i: (b, 0, p)),
        ],
        out_specs=pl.BlockSpec((1, tq, LANES), lambda b, p, i: (b, i, p)),
        out_shape=jax.ShapeDtypeStruct((B, S, D), BF16),
        compiler_params=_cparams(3),
        name="fox_attention",
    )(q, z, k, v)


def _pad_rows(m, rows):
    return jnp.pad(m, ((0, rows - m.shape[0]), (0, 0)))


def kernel(x, meta_tokens, a_norm, a_w_in, a_shift_mu, a_vres_down, a_vres_mu, a_vres_up, a_vres_bias, a_decay_up, a_decay_bias, a_iclr_up, a_iclr_bias, a_k_k, a_k_a, a_r_k, a_gn_w, a_gn_b, a_w_out, kv_norm, kv_w, kv_f_bias, k_norm, b_norm, b_w_in, b_q_norm, b_w_out):
    B, S, D = x.shape
    H = D // HEAD_DIM
    npair = D // LANES
    n_a = a_w_in.shape[0]
    n_b = b_w_in.shape[0]
    n_pad = FRONT - N_META
    lora_d = a_decay_up.shape[1]
    lora_i = a_iclr_up.shape[1]

    meta = jnp.broadcast_to(meta_tokens.astype(x.dtype)[None], (B, N_META, D))
    h = jnp.concatenate([jnp.zeros((B, n_pad, D), x.dtype), meta, x], axis=1)

    v_first = None
    for l in range(n_a):
        w_in = a_w_in[l]
        mu = a_shift_mu[l]
        w_main = w_in[:, :4 * D].astype(BF16)
        w_lora = jnp.zeros((D, 3 * LANES), F32)
        mu_lora = jnp.zeros((3 * LANES,), F32)
        w_lora = w_lora.at[:, 0:lora_d].set(w_in[:, 4 * D:4 * D + lora_d])
        mu_lora = mu_lora.at[0:lora_d].set(mu[4 * D:4 * D + lora_d])
        w_lora = w_lora.at[:, LANES:LANES + lora_i].set(w_in[:, 4 * D + lora_d:4 * D + lora_d + lora_i])
        mu_lora = mu_lora.at[LANES:LANES + lora_i].set(mu[4 * D + lora_d:4 * D + lora_d + lora_i])
        vup = None
        vres_bias = jnp.zeros((D,), F32)
        if l > 0:
            lora_v = a_vres_down.shape[2]
            w_lora = w_lora.at[:, 2 * LANES:2 * LANES + lora_v].set(a_vres_down[l - 1])
            mu_lora = mu_lora.at[2 * LANES:2 * LANES + lora_v].set(a_vres_mu[l - 1])
            vup = _pad_rows(a_vres_up[l - 1], LANES).astype(BF16)
            vres_bias = a_vres_bias[l - 1]
        proj_main = _proj_shift(h, a_norm[l], w_main, mu[:4 * D], 2 * D)
        proj_lora = _proj_shift(h, a_norm[l], w_lora.astype(BF16), mu_lora, 3 * LANES)
        vec = jnp.stack([a_decay_bias[l], a_iclr_bias[l], vres_bias, a_k_k[l], a_k_a[l],
                         a_r_k[l], a_gn_w[l], a_gn_b[l]], axis=0)
        gated = _wkv(proj_main, proj_lora, v_first if l > 0 else None,
                     _pair_blockdiag(_pad_rows(a_decay_up[l], LANES),
                                     _pad_rows(a_iclr_up[l], LANES)).astype(BF16), vup, vec)
        if l == 0:
            v_first = proj_main
        h = _out_proj(gated, a_w_out[l].astype(BF16), h)

    w_kv = jnp.zeros((D, 2 * D + LANES), F32).at[:, :2 * D + H].set(kv_w).astype(BF16)
    fbias = jnp.zeros((1, LANES), F32).at[0, :H].set(kv_f_bias)
    knorm2 = jnp.concatenate([k_norm, k_norm]).reshape(1, LANES)
    k_sh, v_sh, c_all = _shared_kv(h, kv_norm, w_kv, fbias, knorm2, n_pad)

    c_q = c_all[:, FRONT:]
    h = h[:, FRONT:]
    for j in range(n_b):
        qnorm2 = jnp.concatenate([b_q_norm[j], b_q_norm[j]]).reshape(1, LANES)
        q, z = _qz_proj(h, b_norm[j], b_w_in[j].astype(BF16), qnorm2, c_q)
        attn = _fox_attention(q, z, k_sh, v_sh, n_pad)
        h = _out_proj(attn, b_w_out[j].astype(BF16), h)
    return h
```

```python
import functools

import jax
import jax.numpy as jnp
from jax import lax
from jax.experimental import pallas as pl
from jax.experimental.pallas import tpu as pltpu

N_META = 16
HEAD_DIM = 64
NORM_EPS = 1e-6
GN_EPS = 64e-5
LANES = 128
CHUNK = 64
WKV_UNROLL = 10
WKV_PAIRS = 2
WKV_SKEW = 1
PROJ_COLS = 512
PROJ_ROW_SPLIT = 2
FOX_Q_BLOCK = 2048
FOX_ROW_SLAB = 512
FRONT = 128
VMEM_LIMIT = 56 * 1024 * 1024

F32 = jnp.float32
BF16 = jnp.bfloat16
_NT = (((1,), (1,)), ((), ()))


def _dot(a, b):
    return jnp.dot(a, b, preferred_element_type=F32)


def _dot_nt(a, b):
    return lax.dot_general(a, b, _NT, preferred_element_type=F32)


def _split2(x):
    hi = x.astype(BF16)
    lo = (x - hi.astype(F32)).astype(BF16)
    return hi, lo


def _split3(x):
    hi = x.astype(BF16)
    r1 = x - hi.astype(F32)
    mid = r1.astype(BF16)
    lo = (r1 - mid.astype(F32)).astype(BF16)
    return hi, mid, lo


def _sigmoid(x):
    return 1.0 / (1.0 + jnp.exp(-x))


def _softplus(x):
    return jnp.maximum(x, 0.0) + jnp.log(1.0 + jnp.exp(-jnp.abs(x)))


def _pick_tile(n, cap, mult):
    best = mult
    for t in range(mult, cap + 1, mult):
        if n % t == 0:
            best = t
    return best


def _cparams(n_axes):
    return pltpu.CompilerParams(dimension_semantics=("arbitrary",) * n_axes,
                                vmem_limit_bytes=VMEM_LIMIT)


def _col_tiles(n):
    w = PROJ_COLS if n % PROJ_COLS == 0 else n
    return [slice(c, c + w) for c in range(0, n, w)]


def _rms_rows(x_ref, g_ref, n_split):
    tm = x_ref.shape[1]
    rs = tm // n_split if tm % (16 * n_split) == 0 else tm
    out = []
    for r in range(0, tm, rs):
        x = x_ref[0, r:r + rs, :]
        ms = jnp.mean(x * x, axis=-1, keepdims=True)
        out.append((slice(r, r + rs), (x * lax.rsqrt(ms + NORM_EPS) * g_ref[...]).astype(BF16)))
    return out


def _pipelined(units, matmul, epilogue):
    pending = None
    for u in units:
        p = matmul(u)
        if pending is not None:
            epilogue(*pending)
        pending = (u, p)
    epilogue(*pending)


def _proj_shift_kernel(x_ref, g_ref, w_ref, mu_ref, o_ref, carry_ref):
    @pl.when(pl.program_id(2) == 0)
    def _():
        carry_ref[...] = jnp.zeros_like(carry_ref)

    slabs = _rms_rows(x_ref, g_ref, PROJ_ROW_SPLIT)
    units = [(rows, xn, cols) for rows, xn in slabs for cols in _col_tiles(w_ref.shape[1])]

    def matmul(u):
        _, xn, cols = u
        return _dot(xn, w_ref[:, cols])

    def epilogue(u, p):
        rows, _, cols = u
        prev = pltpu.roll(p, 1, axis=0)
        first = lax.broadcasted_iota(jnp.int32, p.shape, 0) == 0
        prev = jnp.where(first, carry_ref[0:1, cols], prev)
        o_ref[0, rows, cols] = p + mu_ref[:, cols] * (prev - p)
        carry_ref[0:1, cols] = p[p.shape[0] - 1:p.shape[0], :]

    _pipelined(units, matmul, epilogue)


def _proj_shift(h, g, w, mu, tn):
    B, T, D = h.shape
    NW = w.shape[1]
    tm = _pick_tile(T, 640, 64)
    return pl.pallas_call(
        _proj_shift_kernel,
        grid=(NW // tn, B, T // tm),
        in_specs=[
            pl.BlockSpec((1, tm, D), lambda j, b, i: (b, i, 0)),
            pl.BlockSpec((1, D), lambda j, b, i: (0, 0)),
            pl.BlockSpec((D, tn), lambda j, b, i: (0, j)),
            pl.BlockSpec((1, tn), lambda j, b, i: (0, j)),
        ],
        out_specs=pl.BlockSpec((1, tm, tn), lambda j, b, i: (b, i, j)),
        out_shape=jax.ShapeDtypeStruct((B, T, NW), F32),
        scratch_shapes=[pltpu.VMEM((8, tn), F32)],
        compiler_params=_cparams(3),
        name="proj_shift",
    )(h, g.reshape(1, D), w, mu.reshape(1, NW))


def _wkv_kernel(*refs, has_vres, n_chunks, unroll, skew):
    if has_vres:
        (r_ref, k_ref, v_ref, z_ref, wl_ref, al_ref, vl_ref, vf_ref,
         dui_ref, vup_ref, vec_ref, o_ref, h_ref) = refs
    else:
        (r_ref, k_ref, v_ref, z_ref, wl_ref, al_ref,
         dui_ref, vec_ref, o_ref, h_ref) = refs
        vl_ref = vf_ref = vup_ref = None
    C = CHUNK
    L = LANES

    n_batch = r_ref.shape[0]
    n_pairs = r_ref.shape[2] // L
    streams = [(bt, pp) for pp in range(n_pairs) for bt in range(n_batch)]

    @pl.when(pl.program_id(1) == 0)
    def _():
        h_ref[...] = jnp.zeros_like(h_ref)

    ri = lax.broadcasted_iota(jnp.int32, (L, L), 0)
    ci = lax.broadcasted_iota(jnp.int32, (L, L), 1)
    same_head = (ri < C) == (ci < C)
    rt_ = ri & (C - 1)
    ct_ = ci & (C - 1)
    hm = same_head.astype(F32)
    tri_s = same_head & (ct_ < rt_)
    tri_i = same_head & (ct_ <= rt_)
    eye = ri == ci
    eye_f = eye.astype(F32)
    r64 = lax.broadcasted_iota(jnp.int32, (C, C), 0)
    c64 = lax.broadcasted_iota(jnp.int32, (C, C), 1)
    tril64 = (c64 <= r64).astype(BF16)

    pair_vec = [[vec_ref[j:j + 1, pp * L:(pp + 1) * L] for j in range(8)] for pp in range(n_pairs)]
    pair_dui = [dui_ref[:, pp * 2 * L:(pp + 1) * 2 * L] for pp in range(n_pairs)]
    pair_vup = [vup_ref[:, pp * L:(pp + 1) * L] if has_vres else None for pp in range(n_pairs)]

    lane_h0 = lax.broadcasted_iota(jnp.int32, (C, L), 1) < HEAD_DIM

    def segsum(x):
        s0 = jnp.sum(jnp.where(lane_h0, x, 0.0), axis=-1, keepdims=True)
        s1 = jnp.sum(jnp.where(lane_h0, 0.0, x), axis=-1, keepdims=True)
        return jnp.where(lane_h0, s0, s1)

    def stack(x):
        return jnp.concatenate([x, x], axis=0) * hm

    def dup2(x):
        return jnp.concatenate([x, x], axis=0)

    zeros_ll = jnp.zeros((L, L), BF16)

    def chunk_stages(bb, c):
        rows = pl.ds(pl.multiple_of(c * C, C), C)
        bt, pp = streams[bb]
        ls = slice(pp * L, (pp + 1) * L)
        decay_bias, iclr_bias, vres_bias, k_k, k_a, r_k = pair_vec[pp][0:6]
        dui, vup = pair_dui[pp], pair_vup[pp]
        r = r_ref[bt, rows, ls]
        k = k_ref[bt, rows, ls]
        v = v_ref[bt, rows, ls]
        wl = wl_ref[bt, rows, :]
        al = al_ref[bt, rows, :]

        up = _dot(jnp.concatenate([jnp.tanh(wl), al], axis=1).astype(BF16), dui)
        logw = -jnp.exp(-_softplus(-(decay_bias + up[:, 0:L])) - 0.5)
        a = _sigmoid(iclr_bias + up[:, L:2 * L])
        if has_vres:
            vl = vl_ref[bt, rows, :]
            vf = vf_ref[bt, rows, ls]
            v = v + (vf - v) * _sigmoid(vres_bias + _dot(vl.astype(BF16), vup))
        kkf = k * k_k
        yield
        kk = kkf / jnp.maximum(jnp.sqrt(segsum(kkf * kkf)), 1e-12)
        k2 = k * (1.0 + (a - 1.0) * k_a)
        b = kk * a

        cum2 = _dot(tril64, jnp.concatenate(_split2(logw), axis=1))
        cum = cum2[:, 0:L] + cum2[:, L:2 * L]
        yield
        mid = cum[C // 2 - 1:C // 2, :]
        last = cum[C - 1:C, :]
        g_in = jnp.exp(cum - mid)
        g_ex = jnp.exp(cum - logw - mid)
        g_inv = jnp.exp(mid - cum)
        g_end = jnp.exp(last - cum)
        gm = jnp.exp(mid)
        g_c = jnp.exp(last)

        at_m = stack(-kk * g_ex)
        rt_m = stack(r * g_in)
        v_mb = stack(v).astype(BF16)
        lhs = jnp.concatenate([at_m, rt_m], axis=0).astype(BF16)
        rhs = jnp.concatenate([dup2(b * g_inv), dup2(k2 * g_inv)], axis=0).astype(BF16)
        A = _dot_nt(lhs, rhs)
        yield
        a_ab = jnp.where(tri_s, A[0:L, 0:L], 0.0)
        a_ak = jnp.where(tri_s, A[0:L, L:2 * L], 0.0).astype(BF16)
        a_rb = jnp.where(tri_i, A[L:2 * L, 0:L], 0.0).astype(BF16)
        a_rk = jnp.where(tri_i, A[L:2 * L, L:2 * L], 0.0).astype(BF16)
        AV = _dot(a_ak, v_mb)

        Pb = a_ab.astype(BF16)
        Sm = eye_f + a_ab
        Pb = _dot(Pb, Pb).astype(BF16)
        yield
        for _ in range(4):
            PS = _dot(Pb, jnp.concatenate([Pb, Sm.astype(BF16)], axis=1))
            yield
            Pb = PS[:, 0:L].astype(BF16)
            Sm = Sm + PS[:, L:2 * L]
        Sm = Sm + _dot(Pb, Sm.astype(BF16))
        yield

        X = _dot(Sm.astype(BF16),
                 jnp.concatenate([(at_m * gm).astype(BF16), AV.astype(BF16)], axis=1))
        bk = jnp.concatenate([stack(b * g_end), stack(k2 * g_end)], axis=0)
        lhs_big = jnp.concatenate(
            [jnp.concatenate([a_rb, a_rk], axis=1), bk.T.astype(BF16)], axis=0)
        yield
        rhs_big = jnp.concatenate(
            [X.astype(BF16), jnp.concatenate([zeros_ll, v_mb], axis=1)], axis=0)
        big = _dot(lhs_big, rhs_big)
        yield
        rm = big[:, 0:L] + jnp.concatenate([rt_m * gm, eye_f * g_c], axis=0)
        bonus = segsum(r * k2 * r_k) * v
        yield (rows, rm.astype(BF16), big, bonus)

    def epilogue_stages(bb, rows, y, bonus):
        mean = segsum(y) * (1.0 / HEAD_DIM)
        yield
        d = y - mean
        var = segsum(d * d) * (1.0 / HEAD_DIM)
        yield
        bt, pp = streams[bb]
        ls = slice(pp * L, (pp + 1) * L)
        gn_w, gn_b = pair_vec[pp][6:8]
        yn = d * lax.rsqrt(var + GN_EPS) * gn_w + gn_b
        z = z_ref[bt, rows, ls]
        out = (yn + bonus) * (z * _sigmoid(z))
        o_ref[bt, rows, ls] = out.astype(o_ref.dtype)
        yield True

    def full_chunk(bb, c, state):
        gen = chunk_stages(bb, c)
        out = next(gen)
        while out is None:
            yield
            out = next(gen)
        rows, rm_b, big, bonus = out
        Hb = state["H"].astype(BF16)
        Hn = _dot(rm_b[L:2 * L], Hb) + big[L:2 * L, L:2 * L]
        state["H"] = Hn
        yield
        Ym = _dot(rm_b[0:L], Hb) + big[0:L, L:2 * L]
        yield
        yield from epilogue_stages(bb, rows, Ym[0:C] + Ym[C:2 * C], bonus)

    def body(i, carry):
        n_streams = len(streams)
        states = [{"H": h_ref[bb]} for bb in range(n_streams)]
        gens = [full_chunk(bb, i * unroll + u, states[bb])
                for u in range(unroll) for bb in range(n_streams)]
        done = [False] * len(gens)
        t = 0
        while not all(done):
            for g, gen in enumerate(gens):
                if not done[g] and t >= skew * g:
                    done[g] = next(gen) is True
            t += 1
        for bb in range(n_streams):
            h_ref[bb] = states[bb]["H"]
        return carry

    lax.fori_loop(0, n_chunks // unroll, body, 0)


def _pair_blockdiag(top, bot):
    npair = top.shape[1] // LANES
    t = top.reshape(LANES, npair, LANES)
    b = bot.reshape(LANES, npair, LANES)
    z = jnp.zeros_like(t)
    rows = [jnp.concatenate([t, z], axis=2), jnp.concatenate([z, b], axis=2)]
    return jnp.concatenate(rows, axis=0).reshape(2 * LANES, 2 * npair * LANES)


def _wkv(proj_main, proj_lora, v_first, dui, vup, vec):
    B, T, D4 = proj_main.shape
    D = D4 // 4
    npair = D // LANES
    tb = _pick_tile(T, 640, CHUNK)
    has_vres = v_first is not None
    pp = WKV_PAIRS
    wide = pp * LANES
    row_blk = lambda off: pl.BlockSpec((B, tb, wide), lambda p, t, off=off: (0, t, off // pp + p))
    lora_blk = lambda j: pl.BlockSpec((B, tb, LANES), lambda p, t, j=j: (0, t, j))
    up_blk = pl.BlockSpec((LANES, wide), lambda p, t: (0, p))
    in_specs = [row_blk(0), row_blk(npair), row_blk(2 * npair), row_blk(3 * npair),
                lora_blk(0), lora_blk(1)]
    args = [proj_main, proj_main, proj_main, proj_main, proj_lora, proj_lora]
    if has_vres:
        in_specs += [lora_blk(2), row_blk(2 * npair)]
        args += [proj_lora, v_first]
    in_specs += [pl.BlockSpec((2 * LANES, 2 * wide), lambda p, t: (0, p))]
    args += [dui]
    if has_vres:
        in_specs += [up_blk]
        args += [vup]
    in_specs += [pl.BlockSpec((8, wide), lambda p, t: (0, p))]
    args += [vec]
    return pl.pallas_call(
        functools.partial(_wkv_kernel, has_vres=has_vres, n_chunks=tb // CHUNK,
                          unroll=_pick_tile(tb // CHUNK, WKV_UNROLL, 1), skew=WKV_SKEW),
        grid=(npair // pp, T // tb),
        in_specs=in_specs,
        out_specs=pl.BlockSpec((B, tb, wide), lambda p, t: (0, t, p)),
        out_shape=jax.ShapeDtypeStruct((B, T, D), BF16),
        scratch_shapes=[pltpu.VMEM((B * pp, LANES, LANES), F32)],
        compiler_params=_cparams(2),
        name="wkv7",
    )(*args)


def _out_proj_kernel(a_ref, w_ref, h_ref, o_ref):
    def epilogue(cols, p):
        o_ref[0, :, cols] = h_ref[0, :, cols] + p

    _pipelined(_col_tiles(w_ref.shape[1]), lambda cols: _dot(a_ref[0], w_ref[:, cols]), epilogue)


def _out_proj(act, w, h):
    B, T, D = h.shape
    tm = _pick_tile(T, 640, 64)
    return pl.pallas_call(
        _out_proj_kernel,
        grid=(B, T // tm),
        in_specs=[
            pl.BlockSpec((1, tm, D), lambda b, i: (b, i, 0)),
            pl.BlockSpec((D, D), lambda b, i: (0, 0)),
            pl.BlockSpec((1, tm, D), lambda b, i: (b, i, 0)),
        ],
        out_specs=pl.BlockSpec((1, tm, D), lambda b, i: (b, i, 0)),
        out_shape=jax.ShapeDtypeStruct((B, T, D), F32),
        compiler_params=_cparams(2),
        name="out_proj",
    )(act, w, h)


def _pair_rms(xg, gain2, ones_bd, scale):
    hi, lo = _split2(xg * xg)
    ms = (_dot(hi, ones_bd) + _dot(lo, ones_bd)) * (1.0 / HEAD_DIM)
    return xg * lax.rsqrt(ms + NORM_EPS) * (gain2 * scale)


def _ones_bd():
    ri = lax.broadcasted_iota(jnp.int32, (LANES, LANES), 0)
    ci = lax.broadcasted_iota(jnp.int32, (LANES, LANES), 1)
    return ((ri < HEAD_DIM) == (ci < HEAD_DIM)).astype(BF16)


BIAS_LANE = HEAD_DIM
N_SPLIT = 3
LOG2E = 1.4426950408889634


def _bias_select(n_heads, first_lane, sign):
    m = jnp.zeros((N_SPLIT * LANES, n_heads * LANES), F32)
    for j in range(N_SPLIT):
        rows = j * LANES + jnp.arange(n_heads)
        cols = jnp.arange(n_heads) * LANES + first_lane + j
        m = m.at[rows, cols].set(sign)
    return m.astype(BF16)


def _bias_lanes(c, select):
    hi, mid, lo = _split3(c * LOG2E)
    return _dot(jnp.concatenate([hi, mid, lo], axis=1), select)


def _lane_range(lo, hi):
    lane = lax.broadcasted_iota(jnp.int32, (1, LANES), 1)
    return ((lane >= lo) & (lane < hi)).astype(F32)


def _store_heads(ref, rows, pair, vals, spare):
    head_lanes = lax.broadcasted_iota(jnp.int32, (1, LANES), 1) < HEAD_DIM
    for hh in range(2):
        cols = slice((2 * pair + hh) * LANES, (2 * pair + hh + 1) * LANES)
        vh = vals if hh == 0 else pltpu.roll(vals, HEAD_DIM, axis=1)
        ref[0, rows, cols] = jnp.where(head_lanes, vh, spare(cols)).astype(ref.dtype)


def _kv_kernel(x_ref, g_ref, w_ref, fb_ref, kn_ref, tril_ref, sel_ref, k_ref, v_ref, c_ref,
               carry_ref, *, n_pad):
    i = pl.program_id(1)
    tm, D = x_ref.shape[1], x_ref.shape[2]

    @pl.when(i == 0)
    def _():
        carry_ref[...] = jnp.zeros_like(carry_ref)

    k_const = _lane_range(BIAS_LANE, BIAS_LANE + N_SPLIT)
    v_const = _lane_range(BIAS_LANE, BIAS_LANE + 1)
    ones_bd = _ones_bd()
    units = []
    for rows, xn in _rms_rows(x_ref, g_ref, PROJ_ROW_SPLIT):
        n = rows.stop - rows.start
        row = lax.broadcasted_iota(jnp.int32, (n, LANES), 0) + (i * tm + rows.start)
        logf = -_softplus(-(_dot(xn, w_ref[:, 2 * D:2 * D + LANES]) + fb_ref[...]))
        logf = jnp.where(row >= n_pad, logf, 0.0)
        tril = tril_ref[0:n, 0:n]
        c = sum(_dot(tril, piece) for piece in _split3(logf)) + carry_ref[0:1, :]
        c_ref[0, rows, :] = c
        carry_ref[0:1, :] = c[n - 1:n, :]
        bias = _bias_lanes(c, sel_ref[...])
        units += [(rows, xn, cols, bias) for cols in _col_tiles(2 * D)]

    def matmul(u):
        _, xn, cols, _ = u
        return _dot(xn, w_ref[:, cols])

    def epilogue(u, p):
        rows, _, cols, bias = u
        for g in range(p.shape[1] // LANES):
            pg = p[:, g * LANES:(g + 1) * LANES]
            pair = (cols.start % D) // LANES + g
            if cols.start < D:
                _store_heads(k_ref, rows, pair, _pair_rms(pg, kn_ref[...], ones_bd, 1.0),
                             lambda hc: bias[:, hc] + k_const)
            else:
                _store_heads(v_ref, rows, pair, pg, lambda hc: v_const)

    _pipelined(units, matmul, epilogue)


def _shared_kv(h, g, w, fbias, knorm2, n_pad):
    B, T, D = h.shape
    NW = w.shape[1]
    H = D // HEAD_DIM
    tm = _pick_tile(T, 640, 64)
    tril = jnp.tril(jnp.ones((tm, tm), BF16))
    sel = _bias_select(H, BIAS_LANE + N_SPLIT, -1.0)
    return pl.pallas_call(
        functools.partial(_kv_kernel, n_pad=n_pad),
        grid=(B, T // tm),
        in_specs=[
            pl.BlockSpec((1, tm, D), lambda b, i: (b, i, 0)),
            pl.BlockSpec((1, D), lambda b, i: (0, 0)),
            pl.BlockSpec((D, NW), lambda b, i: (0, 0)),
            pl.BlockSpec((1, LANES), lambda b, i: (0, 0)),
            pl.BlockSpec((1, LANES), lambda b, i: (0, 0)),
            pl.BlockSpec((tm, tm), lambda b, i: (0, 0)),
            pl.BlockSpec((N_SPLIT * LANES, H * LANES), lambda b, i: (0, 0)),
        ],
        out_specs=[
            pl.BlockSpec((1, tm, H * LANES), lambda b, i: (b, i, 0)),
            pl.BlockSpec((1, tm, H * LANES), lambda b, i: (b, i, 0)),
            pl.BlockSpec((1, tm, LANES), lambda b, i: (b, i, 0)),
        ],
        out_shape=[
            jax.ShapeDtypeStruct((B, T, H * LANES), BF16),
            jax.ShapeDtypeStruct((B, T, H * LANES), BF16),
            jax.ShapeDtypeStruct((B, T, LANES), F32),
        ],
        scratch_shapes=[pltpu.VMEM((8, LANES), F32)],
        compiler_params=_cparams(2),
        name="shared_kv",
    )(h, g.reshape(1, D), w, fbias, knorm2, tril, sel)


def _qz_kernel(x_ref, g_ref, w_ref, qn_ref, c_ref, sel_ref, q_ref, z_ref):
    D = x_ref.shape[2]
    q_const = _lane_range(BIAS_LANE + N_SPLIT, BIAS_LANE + 2 * N_SPLIT)
    ones_bd = _ones_bd()
    units = []
    for rows, xn in _rms_rows(x_ref, g_ref, PROJ_ROW_SPLIT):
        bias = _bias_lanes(c_ref[0, rows, :], sel_ref[...])
        units += [(rows, xn, cols, bias) for cols in _col_tiles(2 * D)]

    def matmul(u):
        _, xn, cols, _ = u
        return _dot(xn, w_ref[:, cols])

    def epilogue(u, p):
        rows, _, cols, bias = u
        if cols.start >= D:
            z_ref[0, rows, cols.start - D:cols.stop - D] = p
            return
        for g in range(p.shape[1] // LANES):
            qn = _pair_rms(p[:, g * LANES:(g + 1) * LANES], qn_ref[...], ones_bd,
                           HEAD_DIM ** -0.5 * LOG2E)
            _store_heads(q_ref, rows, cols.start // LANES + g, qn,
                         lambda hc: bias[:, hc] + q_const)

    _pipelined(units, matmul, epilogue)


def _qz_proj(h, g, w, qnorm2, c_q):
    B, S, D = h.shape
    H = D // HEAD_DIM
    tm = _pick_tile(S, 512, 64)
    sel = _bias_select(H, BIAS_LANE, 1.0)
    return pl.pallas_call(
        _qz_kernel,
        grid=(B, S // tm),
        in_specs=[
            pl.BlockSpec((1, tm, D), lambda b, i: (b, i, 0)),
            pl.BlockSpec((1, D), lambda b, i: (0, 0)),
            pl.BlockSpec((D, 2 * D), lambda b, i: (0, 0)),
            pl.BlockSpec((1, LANES), lambda b, i: (0, 0)),
            pl.BlockSpec((1, tm, LANES), lambda b, i: (b, i, 0)),
            pl.BlockSpec((N_SPLIT * LANES, H * LANES), lambda b, i: (0, 0)),
        ],
        out_specs=[
            pl.BlockSpec((1, tm, H * LANES), lambda b, i: (b, i, 0)),
            pl.BlockSpec((1, tm, D), lambda b, i: (b, i, 0)),
        ],
        out_shape=[
            jax.ShapeDtypeStruct((B, S, H * LANES), BF16),
            jax.ShapeDtypeStruct((B, S, D), F32),
        ],
        compiler_params=_cparams(2),
        name="qz_proj",
    )(h, g.reshape(1, D), w, qnorm2, c_q, sel)


def _fox_kernel(q_ref, z_ref, k_ref, v_ref, o_ref, *, tq, rs, n_pad):
    qi = pl.program_id(2)
    n_slab = tq // rs
    lane = lax.broadcasted_iota(jnp.int32, (1, LANES), 1)
    col_m = lax.broadcasted_iota(jnp.int32, (1, FRONT), 1)
    causal = (lax.broadcasted_iota(jnp.int32, (rs, rs), 1)
              <= lax.broadcasted_iota(jnp.int32, (rs, rs), 0))
    neg = -jnp.inf

    def row_max(s):
        t = s[:, 0:LANES]
        for j in range(1, s.shape[1] // LANES):
            t = jnp.maximum(t, s[:, j * LANES:(j + 1) * LANES])
        return jnp.max(t, axis=-1, keepdims=True)

    def online(carry, s, vs):
        m, acc = carry
        m_new = jnp.maximum(m, row_max(s))
        p = jnp.exp2((s - m_new).astype(BF16))
        acc = acc * jnp.exp2(m - m_new) + _dot(p, vs)
        return m_new, acc

    chains = [(slice(hh * LANES, (hh + 1) * LANES), r) for hh in range(2) for r in range(n_slab)]
    qs = [q_ref[0, r * rs:(r + 1) * rs, cols] for cols, r in chains]

    def visit(carry, visits):
        ss = [_dot_nt(qs[c], k_ref[0, rows, chains[c][0]]) for c, rows, _ in visits]
        carry = list(carry)
        for s, (c, rows, mask) in zip(ss, visits):
            if mask is not None:
                s = jnp.where(mask, s, neg)
            carry[c] = online(carry[c], s, v_ref[0, rows, chains[c][0]])
        return tuple(carry)

    def key_rows(kb):
        return pl.ds(pl.multiple_of(FRONT + kb * rs, LANES), rs)

    everyone = range(len(chains))

    def body(kb, carry):
        for j in range(n_slab):
            carry = visit(carry, [(c, key_rows(kb * n_slab + j), None) for c in everyone])
        return carry

    carry = tuple((jnp.full((rs, 1), neg, F32), jnp.zeros((rs, LANES), F32)) for _ in everyone)
    carry = lax.fori_loop(0, qi, body, carry)

    tail = [(c, slice(0, FRONT), col_m >= n_pad) for c in everyone]
    for j in range(n_slab):
        tail += [(c, key_rows(qi * n_slab + j), causal if r == j else None)
                 for c, (_, r) in enumerate(chains) if r >= j]
    carry = visit(carry, tail)

    outs = []
    for hh in range(2):
        acc = jnp.concatenate([carry[c][1] for c, (cols, _) in enumerate(chains)
                               if cols.start == hh * LANES], axis=0)
        outs.append(acc / acc[:, BIAS_LANE:BIAS_LANE + 1])

    z = z_ref[0]
    out = jnp.where(lane < HEAD_DIM, outs[0], pltpu.roll(outs[1], HEAD_DIM, axis=1))
    o_ref[0] = (out * (z * _sigmoid(z))).astype(o_ref.dtype)


def _fox_attention(q, z, k, v, n_pad):
    B, S, D = z.shape
    T = k.shape[1]
    npair = D // LANES
    tq = _pick_tile(S, FOX_Q_BLOCK, 128)
    rs = _pick_tile(tq, FOX_ROW_SLAB, 128)
    return pl.pallas_call(
        functools.partial(_fox_kernel, tq=tq, rs=rs, n_pad=n_pad),
        grid=(B, npair, S // tq),
        in_specs=[
            pl.BlockSpec((1, tq, 2 * LANES), lambda b, p, i: (b, i, p)),
            pl.BlockSpec((1, tq, LANES), lambda b, p, i: (b, i, p)),
            pl.BlockSpec((1, T, 2 * LANES), lambda b, p, i: (b, 0, p)),
            pl.BlockSpec((1, T, 2 * LANES), lambda b, p, i: (b, 0, p)),
        ],
        out_specs=pl.BlockSpec((1, tq, LANES), lambda b, p, i: (b, i, p)),
        out_shape=jax.ShapeDtypeStruct((B, S, D), BF16),
        compiler_params=_cparams(3),
        name="fox_attention",
    )(q, z, k, v)


def _pad_rows(m, rows):
    return jnp.pad(m, ((0, rows - m.shape[0]), (0, 0)))


def kernel(x, meta_tokens, a_norm, a_w_in, a_shift_mu, a_vres_down, a_vres_mu, a_vres_up, a_vres_bias, a_decay_up, a_decay_bias, a_iclr_up, a_iclr_bias, a_k_k, a_k_a, a_r_k, a_gn_w, a_gn_b, a_w_out, kv_norm, kv_w, kv_f_bias, k_norm, b_norm, b_w_in, b_q_norm, b_w_out):
    B, S, D = x.shape
    H = D // HEAD_DIM
    npair = D // LANES
    n_a = a_w_in.shape[0]
    n_b = b_w_in.shape[0]
    n_pad = FRONT - N_META
    lora_d = a_decay_up.shape[1]
    lora_i = a_iclr_up.shape[1]

    meta = jnp.broadcast_to(meta_tokens.astype(x.dtype)[None], (B, N_META, D))
    h = jnp.concatenate([jnp.zeros((B, n_pad, D), x.dtype), meta, x], axis=1)

    v_first = None
    for l in range(n_a):
        w_in = a_w_in[l]
        mu = a_shift_mu[l]
        w_main = w_in[:, :4 * D].astype(BF16)
        w_lora = jnp.zeros((D, 3 * LANES), F32)
        mu_lora = jnp.zeros((3 * LANES,), F32)
        w_lora = w_lora.at[:, 0:lora_d].set(w_in[:, 4 * D:4 * D + lora_d])
        mu_lora = mu_lora.at[0:lora_d].set(mu[4 * D:4 * D + lora_d])
        w_lora = w_lora.at[:, LANES:LANES + lora_i].set(w_in[:, 4 * D + lora_d:4 * D + lora_d + lora_i])
        mu_lora = mu_lora.at[LANES:LANES + lora_i].set(mu[4 * D + lora_d:4 * D + lora_d + lora_i])
        vup = None
        vres_bias = jnp.zeros((D,), F32)
        if l > 0:
            lora_v = a_vres_down.shape[2]
            w_lora = w_lora.at[:, 2 * LANES:2 * LANES + lora_v].set(a_vres_down[l - 1])
            mu_lora = mu_lora.at[2 * LANES:2 * LANES + lora_v].set(a_vres_mu[l - 1])
            vup = _pad_rows(a_vres_up[l - 1], LANES).astype(BF16)
            vres_bias = a_vres_bias[l - 1]
        proj_main = _proj_shift(h, a_norm[l], w_main, mu[:4 * D], 2 * D)
        proj_lora = _proj_shift(h, a_norm[l], w_lora.astype(BF16), mu_lora, 3 * LANES)
        vec = jnp.stack([a_decay_bias[l], a_iclr_bias[l], vres_bias, a_k_k[l], a_k_a[l],
                         a_r_k[l], a_gn_w[l], a_gn_b[l]], axis=0)
        gated = _wkv(proj_main, proj_lora, v_first if l > 0 else None,
                     _pair_blockdiag(_pad_rows(a_decay_up[l], LANES),
                                     _pad_rows(a_iclr_up[l], LANES)).astype(BF16), vup, vec)
        if l == 0:
            v_first = proj_main
        h = _out_proj(gated, a_w_out[l].astype(BF16), h)

    w_kv = jnp.zeros((D, 2 * D + LANES), F32).at[:, :2 * D + H].set(kv_w).astype(BF16)
    fbias = jnp.zeros((1, LANES), F32).at[0, :H].set(kv_f_bias)
    knorm2 = jnp.concatenate([k_norm, k_norm]).reshape(1, LANES)
    k_sh, v_sh, c_all = _shared_kv(h, kv_norm, w_kv, fbias, knorm2, n_pad)

    c_q = c_all[:, FRONT:]
    h = h[:, FRONT:]
    for j in range(n_b):
        qnorm2 = jnp.concatenate([b_q_norm[j], b_q_norm[j]]).reshape(1, LANES)
        q, z = _qz_proj(h, b_norm[j], b_w_in[j].astype(BF16), qnorm2, c_q)
        attn = _fox_attention(q, z, k_sh, v_sh, n_pad)
        h = _out_proj(attn, b_w_out[j].astype(BF16), h)
    return h
```

```python
import functools

import jax
import jax.numpy as jnp
from jax import lax
from jax.experimental import pallas as pl
from jax.experimental.pallas import tpu as pltpu

N_META = 16
HEAD_DIM = 64
NORM_EPS = 1e-6
GN_EPS = 64e-5
LANES = 128
CHUNK = 64
WKV_UNROLL = 10
WKV_PAIRS = 2
WKV_SKEW = 1
PROJ_COLS = 512
PROJ_ROW_SPLIT = 2
FOX_Q_BLOCK = 2048
FOX_ROW_SLAB = 512
FRONT = 128
VMEM_LIMIT = 56 * 1024 * 1024

F32 = jnp.float32
BF16 = jnp.bfloat16
_NT = (((1,), (1,)), ((), ()))


def _dot(a, b):
    return jnp.dot(a, b, preferred_element_type=F32)


def _dot_nt(a, b):
    return lax.dot_general(a, b, _NT, preferred_element_type=F32)


def _split2(x):
    hi = x.astype(BF16)
    lo = (x - hi.astype(F32)).astype(BF16)
    return hi, lo


def _split3(x):
    hi = x.astype(BF16)
    r1 = x - hi.astype(F32)
    mid = r1.astype(BF16)
    lo = (r1 - mid.astype(F32)).astype(BF16)
    return hi, mid, lo


def _sigmoid(x):
    return 1.0 / (1.0 + jnp.exp(-x))


def _softplus(x):
    return jnp.maximum(x, 0.0) + jnp.log(1.0 + jnp.exp(-jnp.abs(x)))


def _pick_tile(n, cap, mult):
    best = mult
    for t in range(mult, cap + 1, mult):
        if n % t == 0:
            best = t
    return best


def _cparams(n_axes):
    return pltpu.CompilerParams(dimension_semantics=("arbitrary",) * n_axes,
                                vmem_limit_bytes=VMEM_LIMIT)


def _col_tiles(n):
    return [slice(c, min(c + PROJ_COLS, n)) for c in range(0, n, PROJ_COLS)]


def _rms_rows(x_ref, g_ref, n_split):
    tm = x_ref.shape[1]
    rs = tm // n_split if tm % (16 * n_split) == 0 else tm
    out = []
    for r in range(0, tm, rs):
        x = x_ref[0, r:r + rs, :]
        ms = jnp.mean(x * x, axis=-1, keepdims=True)
        out.append((slice(r, r + rs), (x * lax.rsqrt(ms + NORM_EPS) * g_ref[...]).astype(BF16)))
    return out


def _pipelined(units, matmul, epilogue):
    pending = None
    for u in units:
        p = matmul(u)
        if pending is not None:
            epilogue(*pending)
        pending = (u, p)
    epilogue(*pending)


def _proj_shift_kernel(x_ref, g_ref, w_ref, mu_ref, o_ref, carry_ref):
    @pl.when(pl.program_id(1) == 0)
    def _():
        carry_ref[...] = jnp.zeros_like(carry_ref)

    slabs = _rms_rows(x_ref, g_ref, PROJ_ROW_SPLIT)
    units = [(rows, xn, cols) for rows, xn in slabs for cols in _col_tiles(w_ref.shape[1])]

    def matmul(u):
        _, xn, cols = u
        return _dot(xn, w_ref[:, cols])

    def epilogue(u, p):
        rows, _, cols = u
        prev = pltpu.roll(p, 1, axis=0)
        first = lax.broadcasted_iota(jnp.int32, p.shape, 0) == 0
        prev = jnp.where(first, carry_ref[0:1, cols], prev)
        o_ref[0, rows, cols] = p + mu_ref[:, cols] * (prev - p)
        carry_ref[0:1, cols] = p[p.shape[0] - 1:p.shape[0], :]

    _pipelined(units, matmul, epilogue)


def _proj_shift(h, g, w, mu):
    B, T, D = h.shape
    NW = w.shape[1]
    tm = _pick_tile(T, 640, 64)
    return pl.pallas_call(
        _proj_shift_kernel,
        grid=(B, T // tm),
        in_specs=[
            pl.BlockSpec((1, tm, D), lambda b, i: (b, i, 0)),
            pl.BlockSpec((1, D), lambda b, i: (0, 0)),
            pl.BlockSpec((D, NW), lambda b, i: (0, 0)),
            pl.BlockSpec((1, NW), lambda b, i: (0, 0)),
        ],
        out_specs=pl.BlockSpec((1, tm, NW), lambda b, i: (b, i, 0)),
        out_shape=jax.ShapeDtypeStruct((B, T, NW), F32),
        scratch_shapes=[pltpu.VMEM((8, NW), F32)],
        compiler_params=_cparams(2),
        name="proj_shift",
    )(h, g.reshape(1, D), w, mu.reshape(1, NW))


def _wkv_kernel(*refs, has_vres, n_chunks, unroll, skew):
    if has_vres:
        (r_ref, k_ref, v_ref, z_ref, wl_ref, al_ref, vl_ref, vf_ref,
         dui_ref, vup_ref, vec_ref, o_ref, h_ref) = refs
    else:
        (r_ref, k_ref, v_ref, z_ref, wl_ref, al_ref,
         dui_ref, vec_ref, o_ref, h_ref) = refs
        vl_ref = vf_ref = vup_ref = None
    C = CHUNK
    L = LANES

    n_batch = r_ref.shape[0]
    n_pairs = r_ref.shape[2] // L
    streams = [(bt, pp) for pp in range(n_pairs) for bt in range(n_batch)]

    @pl.when(pl.program_id(1) == 0)
    def _():
        h_ref[...] = jnp.zeros_like(h_ref)

    ri = lax.broadcasted_iota(jnp.int32, (L, L), 0)
    ci = lax.broadcasted_iota(jnp.int32, (L, L), 1)
    same_head = (ri < C) == (ci < C)
    rt_ = ri & (C - 1)
    ct_ = ci & (C - 1)
    hm = same_head.astype(F32)
    tri_s = same_head & (ct_ < rt_)
    tri_i = same_head & (ct_ <= rt_)
    eye = ri == ci
    eye_f = eye.astype(F32)
    r64 = lax.broadcasted_iota(jnp.int32, (C, C), 0)
    c64 = lax.broadcasted_iota(jnp.int32, (C, C), 1)
    tril64 = (c64 <= r64).astype(BF16)

    pair_vec = [[vec_ref[j:j + 1, pp * L:(pp + 1) * L] for j in range(8)] for pp in range(n_pairs)]
    pair_dui = [dui_ref[:, pp * 2 * L:(pp + 1) * 2 * L] for pp in range(n_pairs)]
    pair_vup = [vup_ref[:, pp * L:(pp + 1) * L] if has_vres else None for pp in range(n_pairs)]

    lane_h0 = lax.broadcasted_iota(jnp.int32, (C, L), 1) < HEAD_DIM

    def segsum(x):
        s0 = jnp.sum(jnp.where(lane_h0, x, 0.0), axis=-1, keepdims=True)
        s1 = jnp.sum(jnp.where(lane_h0, 0.0, x), axis=-1, keepdims=True)
        return jnp.where(lane_h0, s0, s1)

    def stack(x):
        return jnp.concatenate([x, x], axis=0) * hm

    def dup2(x):
        return jnp.concatenate([x, x], axis=0)

    zeros_ll = jnp.zeros((L, L), BF16)

    def chunk_stages(bb, c):
        rows = pl.ds(pl.multiple_of(c * C, C), C)
        bt, pp = streams[bb]
        ls = slice(pp * L, (pp + 1) * L)
        decay_bias, iclr_bias, vres_bias, k_k, k_a, r_k = pair_vec[pp][0:6]
        dui, vup = pair_dui[pp], pair_vup[pp]
        r = r_ref[bt, rows, ls]
        k = k_ref[bt, rows, ls]
        v = v_ref[bt, rows, ls]
        wl = wl_ref[bt, rows, :]
        al = al_ref[bt, rows, :]

        up = _dot(jnp.concatenate([jnp.tanh(wl), al], axis=1).astype(BF16), dui)
        logw = -jnp.exp(-_softplus(-(decay_bias + up[:, 0:L])) - 0.5)
        a = _sigmoid(iclr_bias + up[:, L:2 * L])
        if has_vres:
            vl = vl_ref[bt, rows, :]
            vf = vf_ref[bt, rows, ls]
            v = v + (vf - v) * _sigmoid(vres_bias + _dot(vl.astype(BF16), vup))
        kkf = k * k_k
        yield
        kk = kkf / jnp.maximum(jnp.sqrt(segsum(kkf * kkf)), 1e-12)
        k2 = k * (1.0 + (a - 1.0) * k_a)
        b = kk * a

        cum2 = _dot(tril64, jnp.concatenate(_split2(logw), axis=1))
        cum = cum2[:, 0:L] + cum2[:, L:2 * L]
        yield
        mid = cum[C // 2 - 1:C // 2, :]
        last = cum[C - 1:C, :]
        g_in = jnp.exp(cum - mid)
        g_ex = jnp.exp(cum - logw - mid)
        g_inv = jnp.exp(mid - cum)
        g_end = jnp.exp(last - cum)
        gm = jnp.exp(mid)
        g_c = jnp.exp(last)

        at_m = stack(-kk * g_ex)
        rt_m = stack(r * g_in)
        v_mb = stack(v).astype(BF16)
        lhs = jnp.concatenate([at_m, rt_m], axis=0).astype(BF16)
        rhs = jnp.concatenate([dup2(b * g_inv), dup2(k2 * g_inv)], axis=0).astype(BF16)
        A = _dot_nt(lhs, rhs)
        yield
        a_ab = jnp.where(tri_s, A[0:L, 0:L], 0.0)
        a_ak = jnp.where(tri_s, A[0:L, L:2 * L], 0.0).astype(BF16)
        a_rb = jnp.where(tri_i, A[L:2 * L, 0:L], 0.0).astype(BF16)
        a_rk = jnp.where(tri_i, A[L:2 * L, L:2 * L], 0.0).astype(BF16)
        AV = _dot(a_ak, v_mb)

        Pb = a_ab.astype(BF16)
        Sm = eye_f + a_ab
        Pb = _dot(Pb, Pb).astype(BF16)
        yield
        for _ in range(4):
            PS = _dot(Pb, jnp.concatenate([Pb, Sm.astype(BF16)], axis=1))
            yield
            Pb = PS[:, 0:L].astype(BF16)
            Sm = Sm + PS[:, L:2 * L]
        Sm = Sm + _dot(Pb, Sm.astype(BF16))
        yield

        X = _dot(Sm.astype(BF16),
                 jnp.concatenate([(at_m * gm).astype(BF16), AV.astype(BF16)], axis=1))
        bk = jnp.concatenate([stack(b * g_end), stack(k2 * g_end)], axis=0)
        lhs_big = jnp.concatenate(
            [jnp.concatenate([a_rb, a_rk], axis=1), bk.T.astype(BF16)], axis=0)
        yield
        rhs_big = jnp.concatenate(
            [X.astype(BF16), jnp.concatenate([zeros_ll, v_mb], axis=1)], axis=0)
        big = _dot(lhs_big, rhs_big)
        yield
        rm = big[:, 0:L] + jnp.concatenate([rt_m * gm, eye_f * g_c], axis=0)
        bonus = segsum(r * k2 * r_k) * v
        yield (rows, rm.astype(BF16), big, bonus)

    def epilogue_stages(bb, rows, y, bonus):
        mean = segsum(y) * (1.0 / HEAD_DIM)
        yield
        d = y - mean
        var = segsum(d * d) * (1.0 / HEAD_DIM)
        yield
        bt, pp = streams[bb]
        ls = slice(pp * L, (pp + 1) * L)
        gn_w, gn_b = pair_vec[pp][6:8]
        yn = d * lax.rsqrt(var + GN_EPS) * gn_w + gn_b
        z = z_ref[bt, rows, ls]
        out = (yn + bonus) * (z * _sigmoid(z))
        o_ref[bt, rows, ls] = out.astype(o_ref.dtype)
        yield True

    def full_chunk(bb, c, state):
        gen = chunk_stages(bb, c)
        out = next(gen)
        while out is None:
            yield
            out = next(gen)
        rows, rm_b, big, bonus = out
        Hb = state["H"].astype(BF16)
        Hn = _dot(rm_b[L:2 * L], Hb) + big[L:2 * L, L:2 * L]
        state["H"] = Hn
        yield
        Ym = _dot(rm_b[0:L], Hb) + big[0:L, L:2 * L]
        yield
        yield from epilogue_stages(bb, rows, Ym[0:C] + Ym[C:2 * C], bonus)

    def body(i, carry):
        n_streams = len(streams)
        states = [{"H": h_ref[bb]} for bb in range(n_streams)]
        gens = [full_chunk(bb, i * unroll + u, states[bb])
                for u in range(unroll) for bb in range(n_streams)]
        done = [False] * len(gens)
        t = 0
        while not all(done):
            for g, gen in enumerate(gens):
                if not done[g] and t >= skew * g:
                    done[g] = next(gen) is True
            t += 1
        for bb in range(n_streams):
            h_ref[bb] = states[bb]["H"]
        return carry

    lax.fori_loop(0, n_chunks // unroll, body, 0)


def _pair_blockdiag(top, bot):
    npair = top.shape[1] // LANES
    t = top.reshape(LANES, npair, LANES)
    b = bot.reshape(LANES, npair, LANES)
    z = jnp.zeros_like(t)
    rows = [jnp.concatenate([t, z], axis=2), jnp.concatenate([z, b], axis=2)]
    return jnp.concatenate(rows, axis=0).reshape(2 * LANES, 2 * npair * LANES)


def _wkv(proj, v_first, dui, vup, vec):
    B, T, NW = proj.shape
    D = (NW - 3 * LANES) // 4
    npair = D // LANES
    tb = _pick_tile(T, 640, CHUNK)
    has_vres = v_first is not None
    pp = WKV_PAIRS
    wide = pp * LANES
    row_blk = lambda off: pl.BlockSpec((B, tb, wide), lambda p, t, off=off: (0, t, off // pp + p))
    lora_blk = lambda j: pl.BlockSpec((B, tb, LANES), lambda p, t, j=j: (0, t, 4 * npair + j))
    up_blk = pl.BlockSpec((LANES, wide), lambda p, t: (0, p))
    in_specs = [row_blk(0), row_blk(npair), row_blk(2 * npair), row_blk(3 * npair),
                lora_blk(0), lora_blk(1)]
    args = [proj] * 6
    if has_vres:
        in_specs += [lora_blk(2), row_blk(2 * npair)]
        args += [proj, v_first]
    in_specs += [pl.BlockSpec((2 * LANES, 2 * wide), lambda p, t: (0, p))]
    args += [dui]
    if has_vres:
        in_specs += [up_blk]
        args += [vup]
    in_specs += [pl.BlockSpec((8, wide), lambda p, t: (0, p))]
    args += [vec]
    return pl.pallas_call(
        functools.partial(_wkv_kernel, has_vres=has_vres, n_chunks=tb // CHUNK,
                          unroll=_pick_tile(tb // CHUNK, WKV_UNROLL, 1), skew=WKV_SKEW),
        grid=(npair // pp, T // tb),
        in_specs=in_specs,
        out_specs=pl.BlockSpec((B, tb, wide), lambda p, t: (0, t, p)),
        out_shape=jax.ShapeDtypeStruct((B, T, D), BF16),
        scratch_shapes=[pltpu.VMEM((B * pp, LANES, LANES), F32)],
        compiler_params=_cparams(2),
        name="wkv7",
    )(*args)


def _out_proj_kernel(a_ref, w_ref, h_ref, o_ref):
    def epilogue(cols, p):
        o_ref[0, :, cols] = h_ref[0, :, cols] + p

    _pipelined(_col_tiles(w_ref.shape[1]), lambda cols: _dot(a_ref[0], w_ref[:, cols]), epilogue)


def _out_proj(act, w, h):
    B, T, D = h.shape
    tm = _pick_tile(T, 640, 64)
    return pl.pallas_call(
        _out_proj_kernel,
        grid=(B, T // tm),
        in_specs=[
            pl.BlockSpec((1, tm, D), lambda b, i: (b, i, 0)),
            pl.BlockSpec((D, D), lambda b, i: (0, 0)),
            pl.BlockSpec((1, tm, D), lambda b, i: (b, i, 0)),
        ],
        out_specs=pl.BlockSpec((1, tm, D), lambda b, i: (b, i, 0)),
        out_shape=jax.ShapeDtypeStruct((B, T, D), F32),
        compiler_params=_cparams(2),
        name="out_proj",
    )(act, w, h)


def _pair_rms(xg, gain2, scale):
    head0 = lax.broadcasted_iota(jnp.int32, (1, LANES), 1) < HEAD_DIM
    x2 = xg * xg
    s0 = jnp.sum(jnp.where(head0, x2, 0.0), axis=-1, keepdims=True)
    s1 = jnp.sum(jnp.where(head0, 0.0, x2), axis=-1, keepdims=True)
    ms = jnp.where(head0, s0, s1) * (1.0 / HEAD_DIM)
    return xg * lax.rsqrt(ms + NORM_EPS) * (gain2 * scale)


BIAS_LANE = HEAD_DIM
N_SPLIT = 3
LOG2E = 1.4426950408889634


def _bias_select(n_heads, first_lane, sign):
    assert N_SPLIT * n_heads <= LANES
    m = jnp.zeros((LANES, n_heads * LANES), F32)
    for j in range(N_SPLIT):
        rows = j * n_heads + jnp.arange(n_heads)
        cols = jnp.arange(n_heads) * LANES + first_lane + j
        m = m.at[rows, cols].set(sign)
    return m.astype(BF16)


def _bias_lanes(c, select, n_heads):
    lane = lax.broadcasted_iota(jnp.int32, (1, LANES), 1)
    packed = jnp.zeros_like(c)
    for j, piece in enumerate(_split3(c * LOG2E)):
        pf = piece.astype(F32)
        if j:
            pf = pltpu.roll(pf, j * n_heads, axis=1)
        packed = jnp.where((lane >= j * n_heads) & (lane < (j + 1) * n_heads), pf, packed)
    return _dot(packed.astype(BF16), select)


def _lane_range(lo, hi):
    lane = lax.broadcasted_iota(jnp.int32, (1, LANES), 1)
    return ((lane >= lo) & (lane < hi)).astype(F32)


def _store_heads(ref, rows, pair, vals, spare):
    head_lanes = lax.broadcasted_iota(jnp.int32, (1, LANES), 1) < HEAD_DIM
    for hh in range(2):
        cols = slice((2 * pair + hh) * LANES, (2 * pair + hh + 1) * LANES)
        vh = vals if hh == 0 else pltpu.roll(vals, HEAD_DIM, axis=1)
        ref[0, rows, cols] = jnp.where(head_lanes, vh, spare(cols)).astype(ref.dtype)


def _kv_kernel(x_ref, g_ref, w_ref, fb_ref, kn_ref, tril_ref, sel_ref, k_ref, v_ref, c_ref,
               carry_ref, *, n_pad):
    i = pl.program_id(1)
    tm, D = x_ref.shape[1], x_ref.shape[2]

    @pl.when(i == 0)
    def _():
        carry_ref[...] = jnp.zeros_like(carry_ref)

    k_const = _lane_range(BIAS_LANE, BIAS_LANE + N_SPLIT)
    v_const = _lane_range(BIAS_LANE, BIAS_LANE + 1)
    units = []
    for rows, xn in _rms_rows(x_ref, g_ref, PROJ_ROW_SPLIT):
        n = rows.stop - rows.start
        row = lax.broadcasted_iota(jnp.int32, (n, LANES), 0) + (i * tm + rows.start)
        logf = -_softplus(-(_dot(xn, w_ref[:, 2 * D:2 * D + LANES]) + fb_ref[...]))
        logf = jnp.where(row >= n_pad, logf, 0.0)
        tril = tril_ref[0:n, 0:n]
        c = sum(_dot(tril, piece) for piece in _split3(logf)) + carry_ref[0:1, :]
        c_ref[0, rows, :] = c
        carry_ref[0:1, :] = c[n - 1:n, :]
        bias = _bias_lanes(c, sel_ref[...], D // HEAD_DIM)
        units += [(rows, xn, cols, bias) for cols in _col_tiles(2 * D)]

    def matmul(u):
        _, xn, cols, _ = u
        return _dot(xn, w_ref[:, cols])

    def epilogue(u, p):
        rows, _, cols, bias = u
        for g in range(p.shape[1] // LANES):
            pg = p[:, g * LANES:(g + 1) * LANES]
            pair = (cols.start % D) // LANES + g
            if cols.start < D:
                _store_heads(k_ref, rows, pair, _pair_rms(pg, kn_ref[...], 1.0),
                             lambda hc: bias[:, hc] + k_const)
            else:
                _store_heads(v_ref, rows, pair, pg, lambda hc: v_const)

    _pipelined(units, matmul, epilogue)


def _shared_kv(h, g, w, fbias, knorm2, n_pad):
    B, T, D = h.shape
    NW = w.shape[1]
    H = D // HEAD_DIM
    tm = _pick_tile(T, 640, 64)
    tril = jnp.tril(jnp.ones((tm, tm), BF16))
    sel = _bias_select(H, BIAS_LANE + N_SPLIT, -1.0)
    return pl.pallas_call(
        functools.partial(_kv_kernel, n_pad=n_pad),
        grid=(B, T // tm),
        in_specs=[
            pl.BlockSpec((1, tm, D), lambda b, i: (b, i, 0)),
            pl.BlockSpec((1, D), lambda b, i: (0, 0)),
            pl.BlockSpec((D, NW), lambda b, i: (0, 0)),
            pl.BlockSpec((1, LANES), lambda b, i: (0, 0)),
            pl.BlockSpec((1, LANES), lambda b, i: (0, 0)),
            pl.BlockSpec((tm, tm), lambda b, i: (0, 0)),
            pl.BlockSpec((LANES, H * LANES), lambda b, i: (0, 0)),
        ],
        out_specs=[
            pl.BlockSpec((1, tm, H * LANES), lambda b, i: (b, i, 0)),
            pl.BlockSpec((1, tm, H * LANES), lambda b, i: (b, i, 0)),
            pl.BlockSpec((1, tm, LANES), lambda b, i: (b, i, 0)),
        ],
        out_shape=[
            jax.ShapeDtypeStruct((B, T, H * LANES), BF16),
            jax.ShapeDtypeStruct((B, T, H * LANES), BF16),
            jax.ShapeDtypeStruct((B, T, LANES), F32),
        ],
        scratch_shapes=[pltpu.VMEM((8, LANES), F32)],
        compiler_params=_cparams(2),
        name="shared_kv",
    )(h, g.reshape(1, D), w, fbias, knorm2, tril, sel)


def _qz_kernel(x_ref, g_ref, w_ref, qn_ref, c_ref, sel_ref, q_ref, z_ref):
    D = x_ref.shape[2]
    q_const = _lane_range(BIAS_LANE + N_SPLIT, BIAS_LANE + 2 * N_SPLIT)
    units = []
    for rows, xn in _rms_rows(x_ref, g_ref, PROJ_ROW_SPLIT):
        bias = _bias_lanes(c_ref[0, rows, :], sel_ref[...], D // HEAD_DIM)
        units += [(rows, xn, cols, bias) for cols in _col_tiles(2 * D)]

    def matmul(u):
        _, xn, cols, _ = u
        return _dot(xn, w_ref[:, cols])

    def epilogue(u, p):
        rows, _, cols, bias = u
        if cols.start >= D:
            z_ref[0, rows, cols.start - D:cols.stop - D] = p
            return
        for g in range(p.shape[1] // LANES):
            qn = _pair_rms(p[:, g * LANES:(g + 1) * LANES], qn_ref[...],
                           HEAD_DIM ** -0.5 * LOG2E)
            _store_heads(q_ref, rows, cols.start // LANES + g, qn,
                         lambda hc: bias[:, hc] + q_const)

    _pipelined(units, matmul, epilogue)


def _qz_proj(h, g, w, qnorm2, c_q):
    B, S, D = h.shape
    H = D // HEAD_DIM
    tm = _pick_tile(S, 512, 64)
    sel = _bias_select(H, BIAS_LANE, 1.0)
    return pl.pallas_call(
        _qz_kernel,
        grid=(B, S // tm),
        in_specs=[
            pl.BlockSpec((1, tm, D), lambda b, i: (b, i, 0)),
            pl.BlockSpec((1, D), lambda b, i: (0, 0)),
            pl.BlockSpec((D, 2 * D), lambda b, i: (0, 0)),
            pl.BlockSpec((1, LANES), lambda b, i: (0, 0)),
            pl.BlockSpec((1, tm, LANES), lambda b, i: (b, i, 0)),
            pl.BlockSpec((LANES, H * LANES), lambda b, i: (0, 0)),
        ],
        out_specs=[
            pl.BlockSpec((1, tm, H * LANES), lambda b, i: (b, i, 0)),
            pl.BlockSpec((1, tm, D), lambda b, i: (b, i, 0)),
        ],
        out_shape=[
            jax.ShapeDtypeStruct((B, S, H * LANES), BF16),
            jax.ShapeDtypeStruct((B, S, D), F32),
        ],
        compiler_params=_cparams(2),
        name="qz_proj",
    )(h, g.reshape(1, D), w, qnorm2, c_q, sel)


def _fox_kernel(q_ref, z_ref, k_ref, v_ref, o_ref, *, tq, rs, n_pad):
    qi = pl.program_id(2)
    n_slab = tq // rs
    lane = lax.broadcasted_iota(jnp.int32, (1, LANES), 1)
    col_m = lax.broadcasted_iota(jnp.int32, (1, FRONT), 1)
    causal = (lax.broadcasted_iota(jnp.int32, (rs, rs), 1)
              <= lax.broadcasted_iota(jnp.int32, (rs, rs), 0))
    neg = -jnp.inf

    def row_max(s):
        t = s[:, 0:LANES]
        for j in range(1, s.shape[1] // LANES):
            t = jnp.maximum(t, s[:, j * LANES:(j + 1) * LANES])
        return jnp.max(t, axis=-1, keepdims=True)

    def online(carry, s, vs):
        m, acc = carry
        m_new = jnp.maximum(m, row_max(s))
        p = jnp.exp2((s - m_new).astype(BF16))
        acc = acc * jnp.exp2(m - m_new) + _dot(p, vs)
        return m_new, acc

    chains = [(slice(hh * LANES, (hh + 1) * LANES), r) for hh in range(2) for r in range(n_slab)]
    qs = [q_ref[0, r * rs:(r + 1) * rs, cols] for cols, r in chains]

    def visit(carry, visits):
        ss = [_dot_nt(qs[c], k_ref[0, rows, chains[c][0]]) for c, rows, _ in visits]
        carry = list(carry)
        for s, (c, rows, mask) in zip(ss, visits):
            if mask is not None:
                s = jnp.where(mask, s, neg)
            carry[c] = online(carry[c], s, v_ref[0, rows, chains[c][0]])
        return tuple(carry)

    def key_rows(kb):
        return pl.ds(pl.multiple_of(FRONT + kb * rs, LANES), rs)

    everyone = range(len(chains))

    def body(kb, carry):
        for j in range(n_slab):
            carry = visit(carry, [(c, key_rows(kb * n_slab + j), None) for c in everyone])
        return carry

    carry = tuple((jnp.full((rs, 1), neg, F32), jnp.zeros((rs, LANES), F32)) for _ in everyone)
    carry = lax.fori_loop(0, qi, body, carry)

    tail = [(c, slice(0, FRONT), col_m >= n_pad) for c in everyone]
    for j in range(n_slab):
        tail += [(c, key_rows(qi * n_slab + j), causal if r == j else None)
                 for c, (_, r) in enumerate(chains) if r >= j]
    carry = visit(carry, tail)

    outs = []
    for hh in range(2):
        acc = jnp.concatenate([carry[c][1] for c, (cols, _) in enumerate(chains)
                               if cols.start == hh * LANES], axis=0)
        outs.append(acc / acc[:, BIAS_LANE:BIAS_LANE + 1])

    z = z_ref[0]
    out = jnp.where(lane < HEAD_DIM, outs[0], pltpu.roll(outs[1], HEAD_DIM, axis=1))
    o_ref[0] = (out * (z * _sigmoid(z))).astype(o_ref.dtype)


def _fox_attention(q, z, k, v, n_pad):
    B, S, D = z.shape
    T = k.shape[1]
    npair = D // LANES
    tq = _pick_tile(S, FOX_Q_BLOCK, 128)
    rs = _pick_tile(tq, FOX_ROW_SLAB, 128)
    return pl.pallas_call(
        functools.partial(_fox_kernel, tq=tq, rs=rs, n_pad=n_pad),
        grid=(B, npair, S // tq),
        in_specs=[
            pl.BlockSpec((1, tq, 2 * LANES), lambda b, p, i: (b, i, p)),
            pl.BlockSpec((1, tq, LANES), lambda b, p, i: (b, i, p)),
            pl.BlockSpec((1, T, 2 * LANES), lambda b, p, i: (b, 0, p)),
            pl.BlockSpec((1, T, 2 * LANES), lambda b, p, i: (b, 0, p)),
        ],
        out_specs=pl.BlockSpec((1, tq, LANES), lambda b, p, i: (b, i, p)),
        out_shape=jax.ShapeDtypeStruct((B, S, D), BF16),
        compiler_params=_cparams(3),
        name="fox_attention",
    )(q, z, k, v)


def _pad_rows(m, rows):
    return jnp.pad(m, ((0, rows - m.shape[0]), (0, 0)))


def kernel(x, meta_tokens, a_norm, a_w_in, a_shift_mu, a_vres_down, a_vres_mu, a_vres_up, a_vres_bias, a_decay_up, a_decay_bias, a_iclr_up, a_iclr_bias, a_k_k, a_k_a, a_r_k, a_gn_w, a_gn_b, a_w_out, kv_norm, kv_w, kv_f_bias, k_norm, b_norm, b_w_in, b_q_norm, b_w_out):
    B, S, D = x.shape
    H = D // HEAD_DIM
    npair = D // LANES
    n_a = a_w_in.shape[0]
    n_b = b_w_in.shape[0]
    n_pad = FRONT - N_META
    lora_d = a_decay_up.shape[1]
    lora_i = a_iclr_up.shape[1]

    meta = jnp.broadcast_to(meta_tokens.astype(x.dtype)[None], (B, N_META, D))
    h = jnp.concatenate([jnp.zeros((B, n_pad, D), x.dtype), meta, x], axis=1)

    v_first = None
    for l in range(n_a):
        w_in = a_w_in[l]
        mu = a_shift_mu[l]
        w_main = w_in[:, :4 * D].astype(BF16)
        w_lora = jnp.zeros((D, 3 * LANES), F32)
        mu_lora = jnp.zeros((3 * LANES,), F32)
        w_lora = w_lora.at[:, 0:lora_d].set(w_in[:, 4 * D:4 * D + lora_d])
        mu_lora = mu_lora.at[0:lora_d].set(mu[4 * D:4 * D + lora_d])
        w_lora = w_lora.at[:, LANES:LANES + lora_i].set(w_in[:, 4 * D + lora_d:4 * D + lora_d + lora_i])
        mu_lora = mu_lora.at[LANES:LANES + lora_i].set(mu[4 * D + lora_d:4 * D + lora_d + lora_i])
        vup = None
        vres_bias = jnp.zeros((D,), F32)
        if l > 0:
            lora_v = a_vres_down.shape[2]
            w_lora = w_lora.at[:, 2 * LANES:2 * LANES + lora_v].set(a_vres_down[l - 1])
            mu_lora = mu_lora.at[2 * LANES:2 * LANES + lora_v].set(a_vres_mu[l - 1])
            vup = _pad_rows(a_vres_up[l - 1], LANES).astype(BF16)
            vres_bias = a_vres_bias[l - 1]
        proj = _proj_shift(h, a_norm[l],
                           jnp.concatenate([w_main, w_lora.astype(BF16)], axis=1),
                           jnp.concatenate([mu[:4 * D], mu_lora]))
        vec = jnp.stack([a_decay_bias[l], a_iclr_bias[l], vres_bias, a_k_k[l], a_k_a[l],
                         a_r_k[l], a_gn_w[l], a_gn_b[l]], axis=0)
        gated = _wkv(proj, v_first if l > 0 else None,
                     _pair_blockdiag(_pad_rows(a_decay_up[l], LANES),
                                     _pad_rows(a_iclr_up[l], LANES)).astype(BF16), vup, vec)
        if l == 0:
            v_first = proj
        h = _out_proj(gated, a_w_out[l].astype(BF16), h)

    w_kv = jnp.zeros((D, 2 * D + LANES), F32).at[:, :2 * D + H].set(kv_w).astype(BF16)
    fbias = jnp.zeros((1, LANES), F32).at[0, :H].set(kv_f_bias)
    knorm2 = jnp.concatenate([k_norm, k_norm]).reshape(1, LANES)
    k_sh, v_sh, c_all = _shared_kv(h, kv_norm, w_kv, fbias, knorm2, n_pad)

    c_q = c_all[:, FRONT:]
    h = h[:, FRONT:]
    for j in range(n_b):
        qnorm2 = jnp.concatenate([b_q_norm[j], b_q_norm[j]]).reshape(1, LANES)
        q, z = _qz_proj(h, b_norm[j], b_w_in[j].astype(BF16), qnorm2, c_q)
        attn = _fox_attention(q, z, k_sh, v_sh, n_pad)
        h = _out_proj(attn, b_w_out[j].astype(BF16), h)
    return h
```

```python
import functools

import jax
import jax.numpy as jnp
from jax import lax
from jax.experimental import pallas as pl
from jax.experimental.pallas import tpu as pltpu

N_META = 16
HEAD_DIM = 64
NORM_EPS = 1e-6
GN_EPS = 64e-5
LANES = 128
CHUNK = 64
WKV_UNROLL = 10
WKV_PAIRS = 2
WKV_SKEW = 1
PROJ_COLS = 512
PROJ_ROW_SPLIT = 2
FOX_Q_BLOCK = 2048
FOX_ROW_SLAB = 512
FRONT = 128
VMEM_LIMIT = 56 * 1024 * 1024
FOX_VMEM_LIMIT = 58 * 1024 * 1024

F32 = jnp.float32
BF16 = jnp.bfloat16
_NT = (((1,), (1,)), ((), ()))


def _dot(a, b):
    return jnp.dot(a, b, preferred_element_type=F32)


def _dot_nt(a, b):
    return lax.dot_general(a, b, _NT, preferred_element_type=F32)


def _split2(x):
    hi = x.astype(BF16)
    lo = (x - hi.astype(F32)).astype(BF16)
    return hi, lo


def _split3(x):
    hi = x.astype(BF16)
    r1 = x - hi.astype(F32)
    mid = r1.astype(BF16)
    lo = (r1 - mid.astype(F32)).astype(BF16)
    return hi, mid, lo


def _sigmoid(x):
    return 1.0 / (1.0 + jnp.exp(-x))


def _softplus(x):
    return jnp.maximum(x, 0.0) + jnp.log(1.0 + jnp.exp(-jnp.abs(x)))


def _pick_tile(n, cap, mult):
    best = mult
    for t in range(mult, cap + 1, mult):
        if n % t == 0:
            best = t
    return best


def _cparams(n_axes, vmem_limit=VMEM_LIMIT):
    return pltpu.CompilerParams(dimension_semantics=("arbitrary",) * n_axes,
                                vmem_limit_bytes=vmem_limit)


def _col_tiles(n):
    return [slice(c, min(c + PROJ_COLS, n)) for c in range(0, n, PROJ_COLS)]


def _rms_rows(x_ref, g_ref, n_split):
    tm = x_ref.shape[1]
    rs = tm // n_split if tm % (16 * n_split) == 0 else tm
    out = []
    for r in range(0, tm, rs):
        x = x_ref[0, r:r + rs, :]
        ms = jnp.mean(x * x, axis=-1, keepdims=True)
        out.append((slice(r, r + rs), (x * lax.rsqrt(ms + NORM_EPS) * g_ref[...]).astype(BF16)))
    return out


def _pipelined(units, matmul, epilogue):
    pending = None
    for u in units:
        p = matmul(u)
        if pending is not None:
            epilogue(*pending)
        pending = (u, p)
    epilogue(*pending)


def _proj_shift_kernel(x_ref, g_ref, w_ref, mu_ref, o_ref, carry_ref):
    @pl.when(pl.program_id(1) == 0)
    def _():
        carry_ref[...] = jnp.zeros_like(carry_ref)

    slabs = _rms_rows(x_ref, g_ref, PROJ_ROW_SPLIT)
    units = [(rows, xn, cols) for rows, xn in slabs for cols in _col_tiles(w_ref.shape[1])]

    def matmul(u):
        _, xn, cols = u
        return _dot(xn, w_ref[:, cols])

    def epilogue(u, p):
        rows, _, cols = u
        prev = pltpu.roll(p, 1, axis=0)
        first = lax.broadcasted_iota(jnp.int32, p.shape, 0) == 0
        prev = jnp.where(first, carry_ref[0:1, cols], prev)
        o_ref[0, rows, cols] = p + mu_ref[:, cols] * (prev - p)
        carry_ref[0:1, cols] = p[p.shape[0] - 1:p.shape[0], :]

    _pipelined(units, matmul, epilogue)


def _proj_shift(h, g, w, mu):
    B, T, D = h.shape
    NW = w.shape[1]
    tm = _pick_tile(T, 640, 64)
    return pl.pallas_call(
        _proj_shift_kernel,
        grid=(B, T // tm),
        in_specs=[
            pl.BlockSpec((1, tm, D), lambda b, i: (b, i, 0)),
            pl.BlockSpec((1, D), lambda b, i: (0, 0)),
            pl.BlockSpec((D, NW), lambda b, i: (0, 0)),
            pl.BlockSpec((1, NW), lambda b, i: (0, 0)),
        ],
        out_specs=pl.BlockSpec((1, tm, NW), lambda b, i: (b, i, 0)),
        out_shape=jax.ShapeDtypeStruct((B, T, NW), F32),
        scratch_shapes=[pltpu.VMEM((8, NW), F32)],
        compiler_params=_cparams(2),
        name="proj_shift",
    )(h, g.reshape(1, D), w, mu.reshape(1, NW))


def _wkv_kernel(*refs, has_vres, n_chunks, unroll, skew):
    if has_vres:
        (r_ref, k_ref, v_ref, z_ref, wl_ref, al_ref, vl_ref, vf_ref,
         dui_ref, vup_ref, vec_ref, o_ref, h_ref) = refs
    else:
        (r_ref, k_ref, v_ref, z_ref, wl_ref, al_ref,
         dui_ref, vec_ref, o_ref, h_ref) = refs
        vl_ref = vf_ref = vup_ref = None
    C = CHUNK
    L = LANES

    n_batch = r_ref.shape[0]
    n_pairs = r_ref.shape[2] // L
    streams = [(bt, pp) for pp in range(n_pairs) for bt in range(n_batch)]

    @pl.when(pl.program_id(1) == 0)
    def _():
        h_ref[...] = jnp.zeros_like(h_ref)

    ri = lax.broadcasted_iota(jnp.int32, (L, L), 0)
    ci = lax.broadcasted_iota(jnp.int32, (L, L), 1)
    same_head = (ri < C) == (ci < C)
    rt_ = ri & (C - 1)
    ct_ = ci & (C - 1)
    hm = same_head.astype(F32)
    tri_s = same_head & (ct_ < rt_)
    tri_i = same_head & (ct_ <= rt_)
    eye = ri == ci
    eye_f = eye.astype(F32)
    r64 = lax.broadcasted_iota(jnp.int32, (C, C), 0)
    c64 = lax.broadcasted_iota(jnp.int32, (C, C), 1)
    tril64 = (c64 <= r64).astype(BF16)

    pair_vec = [[vec_ref[j:j + 1, pp * L:(pp + 1) * L] for j in range(8)] for pp in range(n_pairs)]
    pair_dui = [dui_ref[:, pp * 2 * L:(pp + 1) * 2 * L] for pp in range(n_pairs)]
    pair_vup = [vup_ref[:, pp * L:(pp + 1) * L] if has_vres else None for pp in range(n_pairs)]

    lane_h0 = lax.broadcasted_iota(jnp.int32, (C, L), 1) < HEAD_DIM

    def segsum(x):
        s0 = jnp.sum(jnp.where(lane_h0, x, 0.0), axis=-1, keepdims=True)
        s1 = jnp.sum(jnp.where(lane_h0, 0.0, x), axis=-1, keepdims=True)
        return jnp.where(lane_h0, s0, s1)

    def stack(x):
        return jnp.concatenate([x, x], axis=0) * hm

    def dup2(x):
        return jnp.concatenate([x, x], axis=0)

    zeros_ll = jnp.zeros((L, L), BF16)

    def chunk_stages(bb, c):
        rows = pl.ds(pl.multiple_of(c * C, C), C)
        bt, pp = streams[bb]
        ls = slice(pp * L, (pp + 1) * L)
        decay_bias, iclr_bias, vres_bias, k_k, k_a, r_k = pair_vec[pp][0:6]
        dui, vup = pair_dui[pp], pair_vup[pp]
        r = r_ref[bt, rows, ls]
        k = k_ref[bt, rows, ls]
        v = v_ref[bt, rows, ls]
        wl = wl_ref[bt, rows, :]
        al = al_ref[bt, rows, :]

        up = _dot(jnp.concatenate([jnp.tanh(wl), al], axis=1).astype(BF16), dui)
        logw = -jnp.exp(-_softplus(-(decay_bias + up[:, 0:L])) - 0.5)
        a = _sigmoid(iclr_bias + up[:, L:2 * L])
        if has_vres:
            vl = vl_ref[bt, rows, :]
            vf = vf_ref[bt, rows, ls]
            v = v + (vf - v) * _sigmoid(vres_bias + _dot(vl.astype(BF16), vup))
        kkf = k * k_k
        yield
        kk = kkf / jnp.maximum(jnp.sqrt(segsum(kkf * kkf)), 1e-12)
        k2 = k * (1.0 + (a - 1.0) * k_a)
        b = kk * a

        cum2 = _dot(tril64, jnp.concatenate(_split2(logw), axis=1))
        cum = cum2[:, 0:L] + cum2[:, L:2 * L]
        yield
        mid = cum[C // 2 - 1:C // 2, :]
        last = cum[C - 1:C, :]
        g_in = jnp.exp(cum - mid)
        g_ex = jnp.exp(cum - logw - mid)
        g_inv = jnp.exp(mid - cum)
        g_end = jnp.exp(last - cum)
        gm = jnp.exp(mid)
        g_c = jnp.exp(last)

        at_m = stack(-kk * g_ex)
        rt_m = stack(r * g_in)
        v_mb = stack(v).astype(BF16)
        lhs = jnp.concatenate([at_m, rt_m], axis=0).astype(BF16)
        rhs = jnp.concatenate([dup2(b * g_inv), dup2(k2 * g_inv)], axis=0).astype(BF16)
        A = _dot_nt(lhs, rhs)
        yield
        a_ab = jnp.where(tri_s, A[0:L, 0:L], 0.0)
        a_ak = jnp.where(tri_s, A[0:L, L:2 * L], 0.0).astype(BF16)
        a_rb = jnp.where(tri_i, A[L:2 * L, 0:L], 0.0).astype(BF16)
        a_rk = jnp.where(tri_i, A[L:2 * L, L:2 * L], 0.0).astype(BF16)
        AV = _dot(a_ak, v_mb)

        Pb = a_ab.astype(BF16)
        Sm = eye_f + a_ab
        Pb = _dot(Pb, Pb).astype(BF16)
        yield
        for _ in range(4):
            PS = _dot(Pb, jnp.concatenate([Pb, Sm.astype(BF16)], axis=1))
            yield
            Pb = PS[:, 0:L].astype(BF16)
            Sm = Sm + PS[:, L:2 * L]
        Sm = Sm + _dot(Pb, Sm.astype(BF16))
        yield

        X = _dot(Sm.astype(BF16),
                 jnp.concatenate([(at_m * gm).astype(BF16), AV.astype(BF16)], axis=1))
        bk = jnp.concatenate([stack(b * g_end), stack(k2 * g_end)], axis=0)
        lhs_big = jnp.concatenate(
            [jnp.concatenate([a_rb, a_rk], axis=1), bk.T.astype(BF16)], axis=0)
        yield
        rhs_big = jnp.concatenate(
            [X.astype(BF16), jnp.concatenate([zeros_ll, v_mb], axis=1)], axis=0)
        big = _dot(lhs_big, rhs_big)
        yield
        rm = big[:, 0:L] + jnp.concatenate([rt_m * gm, eye_f * g_c], axis=0)
        bonus = segsum(r * k2 * r_k) * v
        yield (rows, rm.astype(BF16), big, bonus)

    def epilogue_stages(bb, rows, y, bonus):
        mean = segsum(y) * (1.0 / HEAD_DIM)
        yield
        d = y - mean
        var = segsum(d * d) * (1.0 / HEAD_DIM)
        yield
        bt, pp = streams[bb]
        ls = slice(pp * L, (pp + 1) * L)
        gn_w, gn_b = pair_vec[pp][6:8]
        yn = d * lax.rsqrt(var + GN_EPS) * gn_w + gn_b
        z = z_ref[bt, rows, ls]
        out = (yn + bonus) * (z * _sigmoid(z))
        o_ref[bt, rows, ls] = out.astype(o_ref.dtype)
        yield True

    def full_chunk(bb, c, state):
        gen = chunk_stages(bb, c)
        out = next(gen)
        while out is None:
            yield
            out = next(gen)
        rows, rm_b, big, bonus = out
        Hb = state["H"].astype(BF16)
        Hn = _dot(rm_b[L:2 * L], Hb) + big[L:2 * L, L:2 * L]
        state["H"] = Hn
        yield
        Ym = _dot(rm_b[0:L], Hb) + big[0:L, L:2 * L]
        yield
        yield from epilogue_stages(bb, rows, Ym[0:C] + Ym[C:2 * C], bonus)

    def body(i, carry):
        n_streams = len(streams)
        states = [{"H": h_ref[bb]} for bb in range(n_streams)]
        gens = [full_chunk(bb, i * unroll + u, states[bb])
                for u in range(unroll) for bb in range(n_streams)]
        done = [False] * len(gens)
        t = 0
        while not all(done):
            for g, gen in enumerate(gens):
                if not done[g] and t >= skew * g:
                    done[g] = next(gen) is True
            t += 1
        for bb in range(n_streams):
            h_ref[bb] = states[bb]["H"]
        return carry

    lax.fori_loop(0, n_chunks // unroll, body, 0)


def _pair_blockdiag(top, bot):
    npair = top.shape[1] // LANES
    t = top.reshape(LANES, npair, LANES)
    b = bot.reshape(LANES, npair, LANES)
    z = jnp.zeros_like(t)
    rows = [jnp.concatenate([t, z], axis=2), jnp.concatenate([z, b], axis=2)]
    return jnp.concatenate(rows, axis=0).reshape(2 * LANES, 2 * npair * LANES)


def _wkv(proj, v_first, dui, vup, vec):
    B, T, NW = proj.shape
    D = (NW - 3 * LANES) // 4
    npair = D // LANES
    tb = _pick_tile(T, 640, CHUNK)
    has_vres = v_first is not None
    pp = WKV_PAIRS
    wide = pp * LANES
    row_blk = lambda off: pl.BlockSpec((B, tb, wide), lambda p, t, off=off: (0, t, off // pp + p))
    lora_blk = lambda j: pl.BlockSpec((B, tb, LANES), lambda p, t, j=j: (0, t, 4 * npair + j))
    up_blk = pl.BlockSpec((LANES, wide), lambda p, t: (0, p))
    in_specs = [row_blk(0), row_blk(npair), row_blk(2 * npair), row_blk(3 * npair),
                lora_blk(0), lora_blk(1)]
    args = [proj] * 6
    if has_vres:
        in_specs += [lora_blk(2), row_blk(2 * npair)]
        args += [proj, v_first]
    in_specs += [pl.BlockSpec((2 * LANES, 2 * wide), lambda p, t: (0, p))]
    args += [dui]
    if has_vres:
        in_specs += [up_blk]
        args += [vup]
    in_specs += [pl.BlockSpec((8, wide), lambda p, t: (0, p))]
    args += [vec]
    return pl.pallas_call(
        functools.partial(_wkv_kernel, has_vres=has_vres, n_chunks=tb // CHUNK,
                          unroll=_pick_tile(tb // CHUNK, WKV_UNROLL, 1), skew=WKV_SKEW),
        grid=(npair // pp, T // tb),
        in_specs=in_specs,
        out_specs=pl.BlockSpec((B, tb, wide), lambda p, t: (0, t, p)),
        out_shape=jax.ShapeDtypeStruct((B, T, D), BF16),
        scratch_shapes=[pltpu.VMEM((B * pp, LANES, LANES), F32)],
        compiler_params=_cparams(2),
        name="wkv7",
    )(*args)


def _out_proj_kernel(a_ref, w_ref, h_ref, o_ref):
    def epilogue(cols, p):
        o_ref[0, :, cols] = h_ref[0, :, cols] + p

    _pipelined(_col_tiles(w_ref.shape[1]), lambda cols: _dot(a_ref[0], w_ref[:, cols]), epilogue)


def _out_proj(act, w, h):
    B, T, D = h.shape
    tm = _pick_tile(T, 640, 64)
    return pl.pallas_call(
        _out_proj_kernel,
        grid=(B, T // tm),
        in_specs=[
            pl.BlockSpec((1, tm, D), lambda b, i: (b, i, 0)),
            pl.BlockSpec((D, D), lambda b, i: (0, 0)),
            pl.BlockSpec((1, tm, D), lambda b, i: (b, i, 0)),
        ],
        out_specs=pl.BlockSpec((1, tm, D), lambda b, i: (b, i, 0)),
        out_shape=jax.ShapeDtypeStruct((B, T, D), F32),
        compiler_params=_cparams(2),
        name="out_proj",
    )(act, w, h)


def _pair_rms(xg, gain2, scale):
    head0 = lax.broadcasted_iota(jnp.int32, (1, LANES), 1) < HEAD_DIM
    x2 = xg * xg
    s0 = jnp.sum(jnp.where(head0, x2, 0.0), axis=-1, keepdims=True)
    s1 = jnp.sum(jnp.where(head0, 0.0, x2), axis=-1, keepdims=True)
    ms = jnp.where(head0, s0, s1) * (1.0 / HEAD_DIM)
    return xg * lax.rsqrt(ms + NORM_EPS) * (gain2 * scale)


BIAS_LANE = HEAD_DIM
N_SPLIT = 3
LOG2E = 1.4426950408889634


def _bias_select(n_heads, first_lane, sign):
    assert N_SPLIT * n_heads <= LANES
    m = jnp.zeros((LANES, n_heads * LANES), F32)
    for j in range(N_SPLIT):
        rows = j * n_heads + jnp.arange(n_heads)
        cols = jnp.arange(n_heads) * LANES + first_lane + j
        m = m.at[rows, cols].set(sign)
    return m.astype(BF16)


def _bias_lanes(c, select, n_heads):
    lane = lax.broadcasted_iota(jnp.int32, (1, LANES), 1)
    packed = jnp.zeros_like(c)
    for j, piece in enumerate(_split3(c * LOG2E)):
        pf = piece.astype(F32)
        if j:
            pf = pltpu.roll(pf, j * n_heads, axis=1)
        packed = jnp.where((lane >= j * n_heads) & (lane < (j + 1) * n_heads), pf, packed)
    return _dot(packed.astype(BF16), select)


def _lane_range(lo, hi):
    lane = lax.broadcasted_iota(jnp.int32, (1, LANES), 1)
    return ((lane >= lo) & (lane < hi)).astype(F32)


def _store_heads(ref, rows, pair, vals, spare):
    head_lanes = lax.broadcasted_iota(jnp.int32, (1, LANES), 1) < HEAD_DIM
    for hh in range(2):
        cols = slice((2 * pair + hh) * LANES, (2 * pair + hh + 1) * LANES)
        vh = vals if hh == 0 else pltpu.roll(vals, HEAD_DIM, axis=1)
        ref[0, rows, cols] = jnp.where(head_lanes, vh, spare(cols)).astype(ref.dtype)


def _kv_kernel(x_ref, g_ref, w_ref, fb_ref, kn_ref, tril_ref, sel_ref, k_ref, v_ref, c_ref,
               carry_ref, *, n_pad):
    i = pl.program_id(1)
    tm, D = x_ref.shape[1], x_ref.shape[2]

    @pl.when(i == 0)
    def _():
        carry_ref[...] = jnp.zeros_like(carry_ref)

    k_const = _lane_range(BIAS_LANE, BIAS_LANE + N_SPLIT)
    v_const = _lane_range(BIAS_LANE, BIAS_LANE + 1)
    units = []
    for rows, xn in _rms_rows(x_ref, g_ref, PROJ_ROW_SPLIT):
        n = rows.stop - rows.start
        row = lax.broadcasted_iota(jnp.int32, (n, LANES), 0) + (i * tm + rows.start)
        logf = -_softplus(-(_dot(xn, w_ref[:, 2 * D:2 * D + LANES]) + fb_ref[...]))
        logf = jnp.where(row >= n_pad, logf, 0.0)
        tril = tril_ref[0:n, 0:n]
        c = sum(_dot(tril, piece) for piece in _split3(logf)) + carry_ref[0:1, :]
        c_ref[0, rows, :] = c
        carry_ref[0:1, :] = c[n - 1:n, :]
        bias = _bias_lanes(c, sel_ref[...], D // HEAD_DIM)
        units += [(rows, xn, cols, bias) for cols in _col_tiles(2 * D)]

    def matmul(u):
        _, xn, cols, _ = u
        return _dot(xn, w_ref[:, cols])

    def epilogue(u, p):
        rows, _, cols, bias = u
        for g in range(p.shape[1] // LANES):
            pg = p[:, g * LANES:(g + 1) * LANES]
            pair = (cols.start % D) // LANES + g
            if cols.start < D:
                _store_heads(k_ref, rows, pair, _pair_rms(pg, kn_ref[...], 1.0),
                             lambda hc: bias[:, hc] + k_const)
            else:
                _store_heads(v_ref, rows, pair, pg, lambda hc: v_const)

    _pipelined(units, matmul, epilogue)


def _shared_kv(h, g, w, fbias, knorm2, n_pad):
    B, T, D = h.shape
    NW = w.shape[1]
    H = D // HEAD_DIM
    tm = _pick_tile(T, 640, 64)
    tril = jnp.tril(jnp.ones((tm, tm), BF16))
    sel = _bias_select(H, BIAS_LANE + N_SPLIT, -1.0)
    return pl.pallas_call(
        functools.partial(_kv_kernel, n_pad=n_pad),
        grid=(B, T // tm),
        in_specs=[
            pl.BlockSpec((1, tm, D), lambda b, i: (b, i, 0)),
            pl.BlockSpec((1, D), lambda b, i: (0, 0)),
            pl.BlockSpec((D, NW), lambda b, i: (0, 0)),
            pl.BlockSpec((1, LANES), lambda b, i: (0, 0)),
            pl.BlockSpec((1, LANES), lambda b, i: (0, 0)),
            pl.BlockSpec((tm, tm), lambda b, i: (0, 0)),
            pl.BlockSpec((LANES, H * LANES), lambda b, i: (0, 0)),
        ],
        out_specs=[
            pl.BlockSpec((1, tm, H * LANES), lambda b, i: (b, i, 0)),
            pl.BlockSpec((1, tm, H * LANES), lambda b, i: (b, i, 0)),
            pl.BlockSpec((1, tm, LANES), lambda b, i: (b, i, 0)),
        ],
        out_shape=[
            jax.ShapeDtypeStruct((B, T, H * LANES), BF16),
            jax.ShapeDtypeStruct((B, T, H * LANES), BF16),
            jax.ShapeDtypeStruct((B, T, LANES), F32),
        ],
        scratch_shapes=[pltpu.VMEM((8, LANES), F32)],
        compiler_params=_cparams(2),
        name="shared_kv",
    )(h, g.reshape(1, D), w, fbias, knorm2, tril, sel)


def _qz_kernel(x_ref, g_ref, w_ref, qn_ref, c_ref, sel_ref, q_ref, z_ref):
    D = x_ref.shape[2]
    q_const = _lane_range(BIAS_LANE + N_SPLIT, BIAS_LANE + 2 * N_SPLIT)
    units = []
    for rows, xn in _rms_rows(x_ref, g_ref, PROJ_ROW_SPLIT):
        bias = _bias_lanes(c_ref[0, rows, :], sel_ref[...], D // HEAD_DIM)
        units += [(rows, xn, cols, bias) for cols in _col_tiles(2 * D)]

    def matmul(u):
        _, xn, cols, _ = u
        return _dot(xn, w_ref[:, cols])

    def epilogue(u, p):
        rows, _, cols, bias = u
        if cols.start >= D:
            z_ref[0, rows, cols.start - D:cols.stop - D] = p
            return
        for g in range(p.shape[1] // LANES):
            qn = _pair_rms(p[:, g * LANES:(g + 1) * LANES], qn_ref[...],
                           HEAD_DIM ** -0.5 * LOG2E)
            _store_heads(q_ref, rows, cols.start // LANES + g, qn,
                         lambda hc: bias[:, hc] + q_const)

    _pipelined(units, matmul, epilogue)


def _qz_proj(h, g, w, qnorm2, c_q):
    B, S, D = h.shape
    H = D // HEAD_DIM
    tm = _pick_tile(S, 512, 64)
    sel = _bias_select(H, BIAS_LANE, 1.0)
    return pl.pallas_call(
        _qz_kernel,
        grid=(B, S // tm),
        in_specs=[
            pl.BlockSpec((1, tm, D), lambda b, i: (b, i, 0)),
            pl.BlockSpec((1, D), lambda b, i: (0, 0)),
            pl.BlockSpec((D, 2 * D), lambda b, i: (0, 0)),
            pl.BlockSpec((1, LANES), lambda b, i: (0, 0)),
            pl.BlockSpec((1, tm, LANES), lambda b, i: (b, i, 0)),
            pl.BlockSpec((LANES, H * LANES), lambda b, i: (0, 0)),
        ],
        out_specs=[
            pl.BlockSpec((1, tm, H * LANES), lambda b, i: (b, i, 0)),
            pl.BlockSpec((1, tm, D), lambda b, i: (b, i, 0)),
        ],
        out_shape=[
            jax.ShapeDtypeStruct((B, S, H * LANES), BF16),
            jax.ShapeDtypeStruct((B, S, D), F32),
        ],
        compiler_params=_cparams(2),
        name="qz_proj",
    )(h, g.reshape(1, D), w, qnorm2, c_q, sel)


def _fox_kernel(q_ref, z_ref, k_ref, v_ref, o_ref, acc_ref, *, tq, rs, n_pad):
    qi = pl.program_id(2)
    n_slab = tq // rs
    lane = lax.broadcasted_iota(jnp.int32, (1, LANES), 1)
    col_m = lax.broadcasted_iota(jnp.int32, (1, FRONT), 1)
    causal = (lax.broadcasted_iota(jnp.int32, (rs, rs), 1)
              <= lax.broadcasted_iota(jnp.int32, (rs, rs), 0))
    neg = -jnp.inf

    def row_max(s):
        t = s[:, 0:LANES]
        for j in range(1, s.shape[1] // LANES):
            t = jnp.maximum(t, s[:, j * LANES:(j + 1) * LANES])
        return jnp.max(t, axis=-1, keepdims=True)

    def online(c, m, s, vs):
        m_new = jnp.maximum(m, row_max(s))
        p = jnp.exp2((s - m_new).astype(BF16))
        acc_ref[c] = acc_ref[c] * jnp.exp2(m - m_new) + _dot(p, vs)
        return m_new

    chains = [(slice(hh * LANES, (hh + 1) * LANES), r) for hh in range(2) for r in range(n_slab)]
    qs = [q_ref[0, r * rs:(r + 1) * rs, cols] for cols, r in chains]

    def visit(ms, groups):
        ms = list(ms)

        def update(visits, ss):
            for s, (c, rows, mask) in zip(ss, visits):
                if mask is not None:
                    s = jnp.where(mask, s, neg)
                ms[c] = online(c, ms[c], s, v_ref[0, rows, chains[c][0]])

        pending = None
        for visits in groups:
            ss = [_dot_nt(qs[c], k_ref[0, rows, chains[c][0]]) for c, rows, _ in visits]
            if pending is not None:
                update(*pending)
            pending = (visits, ss)
        update(*pending)
        return tuple(ms)

    def key_rows(kb):
        return pl.ds(pl.multiple_of(FRONT + kb * rs, LANES), rs)

    everyone = range(len(chains))

    def body(kb, carry):
        for j in range(n_slab):
            carry = visit(carry, [[(c, key_rows(kb * n_slab + j), None) for c in everyone]])
        return carry

    acc_ref[...] = jnp.zeros(acc_ref.shape, F32)
    ms = lax.fori_loop(0, qi, body, tuple(jnp.full((rs, 1), neg, F32) for _ in everyone))

    tail = [[(c, slice(0, FRONT), col_m >= n_pad) for c in everyone]]
    for j in range(n_slab):
        tail += [[(c, key_rows(qi * n_slab + j), causal if r == j else None)
                  for c, (_, r) in enumerate(chains) if r >= j]]
    visit(ms, tail)

    outs = []
    for hh in range(2):
        acc = jnp.concatenate([acc_ref[c] for c, (cols, _) in enumerate(chains)
                               if cols.start == hh * LANES], axis=0)
        outs.append(acc / acc[:, BIAS_LANE:BIAS_LANE + 1])

    z = z_ref[0]
    out = jnp.where(lane < HEAD_DIM, outs[0], pltpu.roll(outs[1], HEAD_DIM, axis=1))
    o_ref[0] = (out * (z * _sigmoid(z))).astype(o_ref.dtype)


def _fox_attention(q, z, k, v, n_pad):
    B, S, D = z.shape
    T = k.shape[1]
    npair = D // LANES
    tq = _pick_tile(S, FOX_Q_BLOCK, 128)
    rs = _pick_tile(tq, FOX_ROW_SLAB, 128)
    return pl.pallas_call(
        functools.partial(_fox_kernel, tq=tq, rs=rs, n_pad=n_pad),
        grid=(B, npair, S // tq),
        in_specs=[
            pl.BlockSpec((1, tq, 2 * LANES), lambda b, p, i: (b, i, p)),
            pl.BlockSpec((1, tq, LANES), lambda b, p, i: (b, i, p)),
            pl.BlockSpec((1, T, 2 * LANES), lambda b, p, i: (b, 0, p)),
            pl.BlockSpec((1, T, 2 * LANES), lambda b, p, i: (b, 0, p)),
        ],
        out_specs=pl.BlockSpec((1, tq, LANES), lambda b, p, i: (b, i, p)),
        out_shape=jax.ShapeDtypeStruct((B, S, D), BF16),
        scratch_shapes=[pltpu.VMEM((2 * tq // rs, rs, LANES), F32)],
        compiler_params=_cparams(3, FOX_VMEM_LIMIT),
        name="fox_attention",
    )(q, z, k, v)


def _pad_rows(m, rows):
    return jnp.pad(m, ((0, rows - m.shape[0]), (0, 0)))


def kernel(x, meta_tokens, a_norm, a_w_in, a_shift_mu, a_vres_down, a_vres_mu, a_vres_up, a_vres_bias, a_decay_up, a_decay_bias, a_iclr_up, a_iclr_bias, a_k_k, a_k_a, a_r_k, a_gn_w, a_gn_b, a_w_out, kv_norm, kv_w, kv_f_bias, k_norm, b_norm, b_w_in, b_q_norm, b_w_out):
    B, S, D = x.shape
    H = D // HEAD_DIM
    npair = D // LANES
    n_a = a_w_in.shape[0]
    n_b = b_w_in.shape[0]
    n_pad = FRONT - N_META
    lora_d = a_decay_up.shape[1]
    lora_i = a_iclr_up.shape[1]

    meta = jnp.broadcast_to(meta_tokens.astype(x.dtype)[None], (B, N_META, D))
    h = jnp.concatenate([jnp.zeros((B, n_pad, D), x.dtype), meta, x], axis=1)

    v_first = None
    for l in range(n_a):
        w_in = a_w_in[l]
        mu = a_shift_mu[l]
        w_main = w_in[:, :4 * D].astype(BF16)
        w_lora = jnp.zeros((D, 3 * LANES), F32)
        mu_lora = jnp.zeros((3 * LANES,), F32)
        w_lora = w_lora.at[:, 0:lora_d].set(w_in[:, 4 * D:4 * D + lora_d])
        mu_lora = mu_lora.at[0:lora_d].set(mu[4 * D:4 * D + lora_d])
        w_lora = w_lora.at[:, LANES:LANES + lora_i].set(w_in[:, 4 * D + lora_d:4 * D + lora_d + lora_i])
        mu_lora = mu_lora.at[LANES:LANES + lora_i].set(mu[4 * D + lora_d:4 * D + lora_d + lora_i])
        vup = None
        vres_bias = jnp.zeros((D,), F32)
        if l > 0:
            lora_v = a_vres_down.shape[2]
            w_lora = w_lora.at[:, 2 * LANES:2 * LANES + lora_v].set(a_vres_down[l - 1])
            mu_lora = mu_lora.at[2 * LANES:2 * LANES + lora_v].set(a_vres_mu[l - 1])
            vup = _pad_rows(a_vres_up[l - 1], LANES).astype(BF16)
            vres_bias = a_vres_bias[l - 1]
        proj = _proj_shift(h, a_norm[l],
                           jnp.concatenate([w_main, w_lora.astype(BF16)], axis=1),
                           jnp.concatenate([mu[:4 * D], mu_lora]))
        vec = jnp.stack([a_decay_bias[l], a_iclr_bias[l], vres_bias, a_k_k[l], a_k_a[l],
                         a_r_k[l], a_gn_w[l], a_gn_b[l]], axis=0)
        gated = _wkv(proj, v_first if l > 0 else None,
                     _pair_blockdiag(_pad_rows(a_decay_up[l], LANES),
                                     _pad_rows(a_iclr_up[l], LANES)).astype(BF16), vup, vec)
        if l == 0:
            v_first = proj
        h = _out_proj(gated, a_w_out[l].astype(BF16), h)

    w_kv = jnp.zeros((D, 2 * D + LANES), F32).at[:, :2 * D + H].set(kv_w).astype(BF16)
    fbias = jnp.zeros((1, LANES), F32).at[0, :H].set(kv_f_bias)
    knorm2 = jnp.concatenate([k_norm, k_norm]).reshape(1, LANES)
    k_sh, v_sh, c_all = _shared_kv(h, kv_norm, w_kv, fbias, knorm2, n_pad)

    c_q = c_all[:, FRONT:]
    h = h[:, FRONT:]
    for j in range(n_b):
        qnorm2 = jnp.concatenate([b_q_norm[j], b_q_norm[j]]).reshape(1, LANES)
        q, z = _qz_proj(h, b_norm[j], b_w_in[j].astype(BF16), qnorm2, c_q)
        attn = _fox_attention(q, z, k_sh, v_sh, n_pad)
        h = _out_proj(attn, b_w_out[j].astype(BF16), h)
    return h
```
